```python
import math
import jax, jax.numpy as jnp
from jax import lax
import numpy as np

D_MODEL = 4096
BATCH = 4
SEQ = 2048
DEPTH = 2
DEC_BATCH = 8
DEC_SEQ = 4
PAST_LEN = 16384
PAGE_SIZE = 128

N_MIXERS = 2
N_GLA_LAYERS = (DEPTH + 1) // 2
N_DIFF_LAYERS = DEPTH // 2
GLA_HEADS = 4
GLA_DK = D_MODEL // 2 // GLA_HEADS
GLA_DV = D_MODEL // GLA_HEADS
GLA_GATE_RANK = 16
GLA_TAU = 16.0
GLA_CHUNK = 64
GLA_IN = 2 * GLA_HEADS * GLA_DK + 2 * GLA_HEADS * GLA_DV + GLA_GATE_RANK
DIFF_HEADS = 16
DIFF_HD = D_MODEL // (2 * DIFF_HEADS)
DIFF_IN = 3 * D_MODEL
Q_BLOCK = 128
REL_BUCKETS = 32
REL_MAX_DIST = 128
D_FF = 11008
DEEPNORM_ALPHA = (2.0 * DEPTH) ** 0.25
DEEPNORM_BETA = (8.0 * DEPTH) ** -0.25
LN_EPS = 1e-5
NEG = -1e30

kernel_name = 'hybrid_gla_diffattn_macaron_step'

F32 = jnp.float32


def _layer_norm(x, g, b=None):
    xf = x.astype(F32)
    xc = xf - jnp.mean(xf, -1, keepdims=True)
    var = jnp.mean(xc * xc, -1, keepdims=True)
    y = xc * lax.rsqrt(var + LN_EPS) * g.astype(F32)
    if b is not None:
        y = y + b.astype(F32)
    return y.astype(x.dtype)


def _rms_norm(x, g):
    xf = x.astype(F32)
    y = xf * lax.rsqrt(jnp.mean(xf * xf, -1, keepdims=True) + LN_EPS) * g.astype(F32)
    return y.astype(x.dtype)


def _swiglu(x, w_gate, w_up, w_down):
    return (jax.nn.silu(x @ w_gate) * (x @ w_up)) @ w_down


def _gla_recurrence(q, k, v, log_a, s0):
    B, T = q.shape[:2]
    C = GLA_CHUNK if T % GLA_CHUNK == 0 else T
    n = T // C

    def to_chunks(t):
        return t.astype(F32).reshape(B, n, C, *t.shape[2:]).swapaxes(0, 1)

    causal = jnp.tril(jnp.ones((C, C), dtype=bool))

    def step(s, inp):
        qc, kc, vc, ac = inp
        b = jnp.cumsum(ac, axis=1)
        b_last = b[:, -1:]
        q_t = qc * jnp.exp(b)
        k_t = kc * jnp.exp(-b)
        att = jnp.where(causal, jnp.einsum('bihk,bjhk->bhij', q_t, k_t), 0.0)
        o = jnp.einsum('bhij,bjhv->bihv', att, vc) + jnp.einsum('bihk,bhkv->bihv', q_t, s)
        k_end = kc * jnp.exp(b_last - b)
        s = jnp.exp(b_last[:, 0])[..., None] * s + jnp.einsum('bjhk,bjhv->bhkv', k_end, vc)
        return s, o

    s_fin, o = lax.scan(step, s0.astype(F32), (to_chunks(q), to_chunks(k), to_chunks(v), to_chunks(log_a)))
    o = o.swapaxes(0, 1).reshape(B, T, q.shape[2], v.shape[-1])
    return o, s_fin


def _gla_mixer(x, w_in, w_a2, b_a, norm_g, w_o, s0):
    B, T, _ = x.shape
    hk, hv = GLA_HEADS * GLA_DK, GLA_HEADS * GLA_DV
    q, k, v, r, g_lr = jnp.split(x @ w_in, [hk, 2 * hk, 2 * hk + hv, 2 * hk + 2 * hv], axis=-1)
    q = q.reshape(B, T, GLA_HEADS, GLA_DK) * (GLA_DK ** -0.5)
    k = k.reshape(B, T, GLA_HEADS, GLA_DK)
    v = v.reshape(B, T, GLA_HEADS, GLA_DV)
    log_a = jax.nn.log_sigmoid((g_lr @ w_a2 + b_a).astype(F32)) / GLA_TAU
    log_a = log_a.reshape(B, T, GLA_HEADS, GLA_DK)
    o, s_fin = _gla_recurrence(q, k, v, log_a, s0)
    o = _layer_norm(o.astype(x.dtype), norm_g)
    o = o.reshape(B, T, hv) * jax.nn.silu(r)
    return o @ w_o, s_fin.astype(x.dtype)


def _t5_bucket(dist):
    n = jnp.maximum(dist, 0)
    max_exact = REL_BUCKETS // 2
    large = max_exact + (jnp.log(jnp.maximum(n, 1).astype(F32) / max_exact)
                         / math.log(REL_MAX_DIST / max_exact) * (REL_BUCKETS - max_exact)).astype(jnp.int32)
    large = jnp.minimum(large, REL_BUCKETS - 1)
    return jnp.where(n < max_exact, n, large)


def _diff_logits(qb, qpos, k, kpos, rel_bias):
    s = jnp.einsum('bqhcd,bkhcd->bchqk', qb, k).astype(F32) * (DIFF_HD ** -0.5)
    dist = qpos[:, None] - kpos[None, :]
    bias = jnp.moveaxis(rel_bias[_t5_bucket(dist)].astype(F32), -1, 0)
    return jnp.where(dist >= 0, s + bias, NEG)


def _diff_attend(q, q_pos, segments, rel_bias, lam):
    B, Tq = q.shape[:2]

    def block(qb, qpos_b):
        logits = [_diff_logits(qb, qpos_b, k, kp, rel_bias) for (k, v, kp) in segments]
        offsets = list(np.cumsum([l.shape[-1] for l in logits])[:-1])
        p = jax.nn.softmax(jnp.concatenate(logits, axis=-1), axis=-1)
        w = p[:, 0] - lam * p[:, 1]
        ws = jnp.split(w, offsets, axis=-1) if offsets else [w]
        out = None
        for wi, (k, v, kp) in zip(ws, segments):
            term = jnp.einsum('bhqk,bkhe->bqhe', wi.astype(v.dtype), v)
            out = term if out is None else out + term
        return out

    if Tq > Q_BLOCK and Tq % Q_BLOCK == 0:
        nb = Tq // Q_BLOCK
        qs = q.reshape(B, nb, Q_BLOCK, *q.shape[2:]).swapaxes(0, 1)
        ps = q_pos.reshape(nb, Q_BLOCK)
        out = lax.map(lambda a: block(a[0], a[1]), (qs, ps))
        return out.swapaxes(0, 1).reshape(B, Tq, DIFF_HEADS, 2 * DIFF_HD)
    return block(q, q_pos)


def _diff_mixer(x, w_in, lam_params, norm_g, w_o, rel_bias, layer_idx, pos0, past):
    B, T, _ = x.shape
    q, k, v = jnp.split(x @ w_in, 3, axis=-1)
    q = q.reshape(B, T, DIFF_HEADS, 2, DIFF_HD)
    k_rows = k.reshape(B, T, DIFF_HEADS, 2 * DIFF_HD)
    v_rows = v.reshape(B, T, DIFF_HEADS, 2 * DIFF_HD)
    lam_init = 0.8 - 0.6 * math.exp(-0.3 * layer_idx)
    lp = lam_params.astype(F32)
    lam = jnp.exp(jnp.sum(lp[0] * lp[1])) - jnp.exp(jnp.sum(lp[2] * lp[3])) + lam_init
    pos = pos0 + jnp.arange(T, dtype=jnp.int32)
    new_seg = (k_rows.reshape(B, T, DIFF_HEADS, 2, DIFF_HD), v_rows, pos)
    if past is None:
        segments = (new_seg,)
    else:
        k_past, v_past = past
        past_pos = jnp.arange(k_past.shape[1], dtype=jnp.int32)
        segments = ((k_past, v_past, past_pos), new_seg)
    o = _diff_attend(q, pos, segments, rel_bias, lam)
    o = _rms_norm(o, norm_g) * (1.0 - lam_init)
    return o.reshape(B, T, D_MODEL) @ w_o, k_rows, v_rows


def setup_inputs(seed: int = 0) -> dict:
    key = jax.random.key(seed)
    ks = jax.random.split(key, 24)
    n_pages = PAST_LEN // PAGE_SIZE
    n_pool = (5 * DEC_BATCH * n_pages) // 4

    def nrm(k, shape, scale):
        return jax.random.normal(k, shape, F32) * scale

    page_table = jax.random.permutation(ks[5], n_pool)[:DEC_BATCH * n_pages]
    page_table = page_table.reshape(DEC_BATCH, n_pages).astype(jnp.int32)
    return {
        'x_prompt': nrm(ks[0], (BATCH, SEQ, D_MODEL), 1.0),
        'x_sample': nrm(ks[1], (DEC_BATCH, DEC_SEQ, D_MODEL), 1.0),
        'state_gla': nrm(ks[2], (N_GLA_LAYERS, DEC_BATCH, GLA_HEADS, GLA_DK, GLA_DV), 0.5),
        'cache_k': nrm(ks[3], (N_DIFF_LAYERS, n_pool, PAGE_SIZE, DIFF_HEADS, 2 * DIFF_HD), 1.0),
        'cache_v': nrm(ks[4], (N_DIFF_LAYERS, n_pool, PAGE_SIZE, DIFF_HEADS, 2 * DIFF_HD), 1.0),
        'page_table': page_table,
        'ln_g': 1.0 + nrm(ks[6], (DEPTH, 3, D_MODEL), 0.02),
        'ln_b': nrm(ks[7], (DEPTH, 3, D_MODEL), 0.02),
        'ffn_w_gate': nrm(ks[8], (DEPTH, 2, D_MODEL, D_FF), D_MODEL ** -0.5),
        'ffn_w_up': nrm(ks[9], (DEPTH, 2, D_MODEL, D_FF), D_MODEL ** -0.5),
        'ffn_w_down': nrm(ks[10], (DEPTH, 2, D_FF, D_MODEL), DEEPNORM_BETA * D_FF ** -0.5),
        'gla_w_in': nrm(ks[11], (N_GLA_LAYERS, D_MODEL, GLA_IN), D_MODEL ** -0.5),
        'gla_w_a2': nrm(ks[12], (N_GLA_LAYERS, GLA_GATE_RANK, GLA_HEADS * GLA_DK), GLA_GATE_RANK ** -0.5),
        'gla_b_a': nrm(ks[13], (N_GLA_LAYERS, GLA_HEADS * GLA_DK), 0.1),
        'gla_norm_g': 1.0 + nrm(ks[14], (N_GLA_LAYERS, GLA_DV), 0.02),
        'gla_w_o': nrm(ks[15], (N_GLA_LAYERS, GLA_HEADS * GLA_DV, D_MODEL), DEEPNORM_BETA * (GLA_HEADS * GLA_DV) ** -0.5),
        'diff_w_in': nrm(ks[16], (N_DIFF_LAYERS, D_MODEL, DIFF_IN), D_MODEL ** -0.5),
        'diff_lambda': nrm(ks[17], (N_DIFF_LAYERS, 4, DIFF_HD), 0.1),
        'diff_norm_g': 1.0 + nrm(ks[18], (N_DIFF_LAYERS, 2 * DIFF_HD), 0.02),
        'diff_w_o': nrm(ks[19], (N_DIFF_LAYERS, D_MODEL, D_MODEL), DEEPNORM_BETA * D_MODEL ** -0.5),
        'rel_bias': nrm(ks[20], (REL_BUCKETS, DIFF_HEADS), 0.5),
    }


def reference(x_prompt, x_sample, state_gla, cache_k, cache_v, page_table, ln_g, ln_b,
              ffn_w_gate, ffn_w_up, ffn_w_down, gla_w_in, gla_w_a2, gla_b_a, gla_norm_g, gla_w_o,
              diff_w_in, diff_lambda, diff_norm_g, diff_w_o, rel_bias):
    n_pages = page_table.shape[1]
    past_len = n_pages * PAGE_SIZE

    def run(x, pos0, gla_init, diff_past):
        gla_states, k_rows, v_rows = [], [], []
        for i in range(DEPTH):
            x = _layer_norm(DEEPNORM_ALPHA * x + 0.5 * _swiglu(x, ffn_w_gate[i, 0], ffn_w_up[i, 0], ffn_w_down[i, 0]),
                            ln_g[i, 0], ln_b[i, 0])
            j = i // N_MIXERS
            if i % N_MIXERS == 0:
                h, s = _gla_mixer(x, gla_w_in[j], gla_w_a2[j], gla_b_a[j], gla_norm_g[j], gla_w_o[j], gla_init(j))
                gla_states.append(s)
            else:
                h, kr, vr = _diff_mixer(x, diff_w_in[j], diff_lambda[j], diff_norm_g[j], diff_w_o[j],
                                        rel_bias, i, pos0, diff_past(j))
                k_rows.append(kr)
                v_rows.append(vr)
            x = _layer_norm(DEEPNORM_ALPHA * x + h, ln_g[i, 1], ln_b[i, 1])
            x = _layer_norm(DEEPNORM_ALPHA * x + 0.5 * _swiglu(x, ffn_w_gate[i, 1], ffn_w_up[i, 1], ffn_w_down[i, 1]),
                            ln_g[i, 2], ln_b[i, 2])
        return x, jnp.stack(gla_states), jnp.stack(k_rows), jnp.stack(v_rows)

    def prompt_init(j):
        return jnp.zeros((x_prompt.shape[0], GLA_HEADS, GLA_DK, GLA_DV), x_prompt.dtype)

    def prompt_past(j):
        return None

    def sample_init(j):
        return state_gla[j]

    def sample_past(j):
        nb = page_table.shape[0]
        k_past = cache_k[j][page_table].reshape(nb, past_len, DIFF_HEADS, 2, DIFF_HD)
        v_past = cache_v[j][page_table].reshape(nb, past_len, DIFF_HEADS, 2 * DIFF_HD)
        return (k_past, v_past)

    y_prompt, gla_state_prompt, k_prompt, v_prompt = run(x_prompt, 0, prompt_init, prompt_past)
    y_sample, gla_state_sample, k_sample, v_sample = run(x_sample, past_len, sample_init, sample_past)
    return (y_prompt, y_sample, gla_state_prompt, gla_state_sample, k_prompt, v_prompt, k_sample, v_sample)
```

```python
import functools
import math

import jax
import jax.numpy as jnp
import numpy as np
from jax import lax
from jax.experimental import pallas as pl
from jax.experimental.pallas import tpu as pltpu

F32 = jnp.float32
BF16 = jnp.bfloat16

DEPTH = 2
N_MIXERS = 2
GLA_HEADS = 4
GLA_GATE_RANK = 16
GLA_TAU = 16.0
GLA_CHUNK = 64
DIFF_HEADS = 16
PAGE_SIZE = 128
REL_BUCKETS = 32
REL_MAX_DIST = 128
DEEPNORM_ALPHA = (2.0 * DEPTH) ** 0.25
LN_EPS = 1e-5
NEG = -1e30

LANES = 128
MIB = 1024 * 1024
SAMPLE_ROWS = 64
SAMPLE_Q_ROWS = 8


def _row_tile(m, cap):
    for t in range(min(m, cap) // 16 * 16, 15, -16):
        if m % t == 0:
            return t
    raise ValueError(f"no row tile for {m} rows")


def _params(sem, vmem_mib):
    return pltpu.CompilerParams(dimension_semantics=sem, vmem_limit_bytes=vmem_mib * MIB)


def _ln_kernel(z_ref, g_ref, b_ref, xf_ref, xb_ref):
    z = z_ref[...]
    zc = z - jnp.mean(z, axis=-1, keepdims=True)
    var = jnp.mean(zc * zc, axis=-1, keepdims=True)
    y = zc * lax.rsqrt(var + LN_EPS) * g_ref[...] + b_ref[...]
    xf_ref[...] = y
    xb_ref[...] = y.astype(BF16)


def _layer_norm(z, g, b):
    m, d = z.shape
    tr = _row_tile(m, 256)
    return pl.pallas_call(
        _ln_kernel,
        grid=(m // tr,),
        in_specs=[pl.BlockSpec((tr, d), lambda i: (i, 0)),
                  pl.BlockSpec((1, d), lambda i: (0, 0)),
                  pl.BlockSpec((1, d), lambda i: (0, 0))],
        out_specs=[pl.BlockSpec((tr, d), lambda i: (i, 0)),
                   pl.BlockSpec((tr, d), lambda i: (i, 0))],
        out_shape=[jax.ShapeDtypeStruct((m, d), F32), jax.ShapeDtypeStruct((m, d), BF16)],
        compiler_params=_params(("arbitrary",), 40),
        name="layer_norm",
    )(z, g.reshape(1, d), b.reshape(1, d))


def _cast_kernel(x_ref, o_ref):
    o_ref[...] = x_ref[...].astype(BF16)


def _to_bf16(x):
    m, d = x.shape
    tr = _row_tile(m, 256)
    return pl.pallas_call(
        _cast_kernel,
        grid=(m // tr,),
        in_specs=[pl.BlockSpec((tr, d), lambda i: (i, 0))],
        out_specs=pl.BlockSpec((tr, d), lambda i: (i, 0)),
        out_shape=jax.ShapeDtypeStruct((m, d), BF16),
        compiler_params=_params(("arbitrary",), 40),
        name="to_bf16",
    )(x)


def _w_spec(w, lead, k, tn, col0):
    none = (None,) * len(lead)
    return pl.BlockSpec(none + (k, tn), lambda n, m: tuple(lead) + (0, col0 + n))


def _proj_kernel(x_ref, w_ref, o_ref, wb_ref):
    @pl.when(pl.program_id(1) == 0)
    def _():
        wb_ref[...] = w_ref[...].astype(BF16)

    o_ref[...] = jnp.dot(x_ref[...], wb_ref[...], preferred_element_type=F32).astype(o_ref.dtype)


def _proj(x_bf, w, lead, col0, n_cols, tn, out_dtype=F32):
    m, k = x_bf.shape
    tm = _row_tile(m, 1024)
    return pl.pallas_call(
        _proj_kernel,
        grid=(n_cols // tn, m // tm),
        in_specs=[pl.BlockSpec((tm, k), lambda n, i: (i, 0)), _w_spec(w, lead, k, tn, col0)],
        out_specs=pl.BlockSpec((tm, tn), lambda n, i: (i, n)),
        out_shape=jax.ShapeDtypeStruct((m, n_cols), out_dtype),
        scratch_shapes=[pltpu.VMEM((k, tn), BF16)],
        compiler_params=_params(("arbitrary", "arbitrary"), 56),
        name="proj",
    )(x_bf, w)


def _ffn_up_kernel(x_ref, wg_ref, wu_ref, h_ref, wgb_ref, wub_ref):
    @pl.when(pl.program_id(1) == 0)
    def _():
        wgb_ref[...] = wg_ref[...].astype(BF16)
        wub_ref[...] = wu_ref[...].astype(BF16)

    x = x_ref[...]
    g = jnp.dot(x, wgb_ref[...], preferred_element_type=F32)
    u = jnp.dot(x, wub_ref[...], preferred_element_type=F32)
    h_ref[...] = (g * jax.nn.sigmoid(g) * u).astype(BF16)


def _ffn_up(x_bf, w_gate, w_up, lead):
    m, k = x_bf.shape
    f = w_gate.shape[-1]
    tm = _row_tile(m, 1024)
    tn = 256
    return pl.pallas_call(
        _ffn_up_kernel,
        grid=(f // tn, m // tm),
        in_specs=[pl.BlockSpec((tm, k), lambda n, i: (i, 0)),
                  _w_spec(w_gate, lead, k, tn, 0), _w_spec(w_up, lead, k, tn, 0)],
        out_specs=pl.BlockSpec((tm, tn), lambda n, i: (i, n)),
        out_shape=jax.ShapeDtypeStruct((m, f), BF16),
        scratch_shapes=[pltpu.VMEM((k, tn), BF16), pltpu.VMEM((k, tn), BF16)],
        compiler_params=_params(("arbitrary", "arbitrary"), 56),
        name="ffn_up",
    )(x_bf, w_gate, w_up)


def _res_mm_kernel(h_ref, w_ref, r_ref, o_ref, wb_ref, *, alpha, scale):
    @pl.when(pl.program_id(1) == 0)
    def _():
        wb_ref[...] = w_ref[...].astype(BF16)

    y = jnp.dot(h_ref[...], wb_ref[...], preferred_element_type=F32)
    o_ref[...] = alpha * r_ref[...] + scale * y


def _res_mm(h_bf, w, lead, res, scale):
    m, k = h_bf.shape
    n_cols = w.shape[-1]
    tn = 256
    tm = _row_tile(m, 512 if k > 8192 else 1024)
    return pl.pallas_call(
        functools.partial(_res_mm_kernel, alpha=DEEPNORM_ALPHA, scale=scale),
        grid=(n_cols // tn, m // tm),
        in_specs=[pl.BlockSpec((tm, k), lambda n, i: (i, 0)), _w_spec(w, lead, k, tn, 0),
                  pl.BlockSpec((tm, tn), lambda n, i: (i, n))],
        out_specs=pl.BlockSpec((tm, tn), lambda n, i: (i, n)),
        out_shape=jax.ShapeDtypeStruct((m, n_cols), F32),
        scratch_shapes=[pltpu.VMEM((k, tn), BF16)],
        compiler_params=_params(("arbitrary", "arbitrary"), 58),
        name="res_mm",
    )(h_bf, w, res)


def _log_sigmoid(x):
    return jnp.minimum(x, 0.0) - jnp.log(1.0 + jnp.exp(-jnp.abs(x)))


def _gla_kernel(*refs, chunk, valid, has_s0, q_scale):
    if has_s0:
        (q_ref, k_ref, v_ref, r_ref, glr_ref, wa2_ref, ba_ref, ng_ref, s0_ref,
         o_ref, sfin_ref, s_ref) = refs
    else:
        (q_ref, k_ref, v_ref, r_ref, glr_ref, wa2_ref, ba_ref, ng_ref,
         o_ref, sfin_ref, s_ref) = refs
    c = pl.program_id(2)

    @pl.when(c == 0)
    def _():
        if has_s0:
            s_ref[...] = s0_ref[...]
        else:
            s_ref[...] = jnp.zeros_like(s_ref)

    dk = q_ref.shape[-1]
    pre = jnp.dot(glr_ref[...].astype(BF16), wa2_ref[...].astype(BF16),
                  preferred_element_type=F32) + ba_ref[...]
    log_a = _log_sigmoid(pre) / GLA_TAU
    k = k_ref[...]
    if valid < chunk:
        row_ok = lax.broadcasted_iota(jnp.int32, (chunk, 1), 0) < valid
        log_a = jnp.where(row_ok, log_a, 0.0)
        k = jnp.where(row_ok, k, 0.0)
    ri = lax.broadcasted_iota(jnp.int32, (chunk, chunk), 0)
    ci = lax.broadcasted_iota(jnp.int32, (chunk, chunk), 1)
    causal = ri >= ci
    b = jnp.dot(jnp.where(causal, 1.0, 0.0), log_a, precision=lax.Precision.HIGHEST,
                preferred_element_type=F32)
    b_last = b[chunk - 1:chunk, :]
    q_t = (q_ref[...] * q_scale * jnp.exp(b)).astype(BF16)
    k_t = (k * jnp.exp(-b)).astype(BF16)
    k_end = (k * jnp.exp(b_last - b)).astype(BF16)
    v = v_ref[...].astype(BF16)
    att = lax.dot_general(q_t, k_t, (((1,), (1,)), ((), ())), preferred_element_type=F32)
    att = jnp.where(causal, att, 0.0).astype(BF16)
    s_old = s_ref[...]
    o = (jnp.dot(att, v, preferred_element_type=F32)
         + jnp.dot(q_t, s_old.astype(BF16), preferred_element_type=F32))
    kv = lax.dot_general(k_end, v, (((0,), (0,)), ((), ())), preferred_element_type=F32)
    decay_col = jnp.exp(jnp.broadcast_to(b_last, (LANES, dk)).T[:, :1])
    s_new = decay_col * s_old + kv
    s_ref[...] = s_new

    @pl.when(c == pl.num_programs(2) - 1)
    def _():
        sfin_ref[...] = s_new

    oc = o - jnp.mean(o, axis=-1, keepdims=True)
    var = jnp.mean(oc * oc, axis=-1, keepdims=True)
    on = oc * lax.rsqrt(var + LN_EPS) * ng_ref[...]
    r = r_ref[...]
    o_ref[...] = (on * (r * jax.nn.sigmoid(r))).astype(BF16)


def _gla_core(qkvr, glr, w_a2p, b_a, norm_g, s0, n_seq, valid):
    m = qkvr.shape[0]
    hk = w_a2p.shape[-1]
    dk = hk // GLA_HEADS
    dv = norm_g.shape[-1]
    t = m // n_seq
    chunk = GLA_CHUNK if t % GLA_CHUNK == 0 else t
    nc = t // chunk
    kb0 = hk // dk
    vb0 = 2 * hk // dv
    rb0 = vb0 + GLA_HEADS
    has_s0 = s0 is not None
    row = lambda b, h, c: b * nc + c
    in_specs = [
        pl.BlockSpec((chunk, dk), lambda b, h, c: (row(b, h, c), h)),
        pl.BlockSpec((chunk, dk), lambda b, h, c: (row(b, h, c), kb0 + h)),
        pl.BlockSpec((chunk, dv), lambda b, h, c: (row(b, h, c), vb0 + h)),
        pl.BlockSpec((chunk, dv), lambda b, h, c: (row(b, h, c), rb0 + h)),
        pl.BlockSpec((chunk, LANES), lambda b, h, c: (row(b, h, c), 0)),
        pl.BlockSpec((LANES, dk), lambda b, h, c: (0, h)),
        pl.BlockSpec((1, dk), lambda b, h, c: (0, h)),
        pl.BlockSpec((1, dv), lambda b, h, c: (0, 0)),
    ]
    args = [qkvr, qkvr, qkvr, qkvr, glr, w_a2p, b_a.reshape(1, hk), norm_g.reshape(1, dv)]
    if has_s0:
        in_specs.append(pl.BlockSpec((None, None, dk, dv), lambda b, h, c: (b, h, 0, 0)))
        args.append(s0)
    return pl.pallas_call(
        functools.partial(_gla_kernel, chunk=chunk, valid=valid, has_s0=has_s0, q_scale=dk ** -0.5),
        grid=(n_seq, GLA_HEADS, nc),
        in_specs=in_specs,
        out_specs=[pl.BlockSpec((chunk, dv), lambda b, h, c: (row(b, h, c), h)),
                   pl.BlockSpec((None, None, dk, dv), lambda b, h, c: (b, h, 0, 0))],
        out_shape=[jax.ShapeDtypeStruct((m, GLA_HEADS * dv), BF16),
                   jax.ShapeDtypeStruct((n_seq, GLA_HEADS, dk, dv), F32)],
        scratch_shapes=[pltpu.VMEM((dk, dv), F32)],
        compiler_params=_params(("arbitrary", "arbitrary", "arbitrary"), 48),
        name="gla_core",
    )(*args)


def _t5_bucket(dist):
    n = jnp.maximum(dist, 0)
    max_exact = REL_BUCKETS // 2
    large = max_exact + (jnp.log(jnp.maximum(n, 1).astype(F32) / max_exact)
                         / math.log(REL_MAX_DIST / max_exact) * (REL_BUCKETS - max_exact)).astype(jnp.int32)
    large = jnp.minimum(large, REL_BUCKETS - 1)
    return jnp.where(n < max_exact, n, large)


def _bias_kernel(qpos_ref, kpos_ref, rb_ref, o_ref):
    bucket = _t5_bucket(qpos_ref[...] - kpos_ref[...])
    rb = rb_ref[...]
    acc = jnp.zeros(o_ref.shape, F32)
    for i in range(REL_BUCKETS):
        acc = jnp.where(bucket == i, rb[:, i:i + 1], acc)
    o_ref[...] = acc


def _bias_table(qpos, kpos, rb_rows, tr, tc):
    r, c = qpos.shape[0], kpos.shape[0]
    return pl.pallas_call(
        _bias_kernel,
        grid=(r // tr, c // tc),
        in_specs=[pl.BlockSpec((tr, 1), lambda i, j: (i, 0)),
                  pl.BlockSpec((1, tc), lambda i, j: (0, j)),
                  pl.BlockSpec((tr, REL_BUCKETS), lambda i, j: (i, 0))],
        out_specs=pl.BlockSpec((tr, tc), lambda i, j: (i, j)),
        out_shape=jax.ShapeDtypeStruct((r, c), F32),
        compiler_params=_params(("arbitrary", "arbitrary"), 32),
        name="t5_bias",
    )(qpos.reshape(r, 1).astype(jnp.int32), kpos.reshape(1, c).astype(jnp.int32), rb_rows)


def _first_const_dist(limit):
    n = np.arange(1, limit + 1)
    large = 16 + (np.log(n.astype(np.float32) / np.float32(16)) / np.float32(math.log(8.0))
                  * np.float32(16)).astype(np.int32)
    bucket = np.where(n < 16, n, np.minimum(large, REL_BUCKETS - 1))
    not_last = np.nonzero(bucket != REL_BUCKETS - 1)[0]
    return int(n[not_last[-1]] + 1) if not_last.size else 1


def _lambda(lam_ref, lam_init):
    lp = lam_ref[...]
    a = jnp.sum(lp[0:1] * lp[1:2], axis=-1, keepdims=True)
    b = jnp.sum(lp[2:3] * lp[3:4], axis=-1, keepdims=True)
    return jnp.exp(a) - jnp.exp(b) + lam_init


def _flash_kernel(q_ref, k_ref, v_ref, bias_ref, lam_ref, ng_ref, o_ref,
                  kb_ref, vb_ref, m_ref, l_ref, acc_ref, *, t, hd, n_near, lam_init, scale):
    qi = pl.program_id(2)

    @pl.when(qi == 0)
    def _():
        kb_ref[...] = k_ref[...].astype(BF16)
        vb_ref[...] = v_ref[...].astype(BF16)

    q = q_ref[...].astype(BF16)
    m_ref[...] = jnp.full(m_ref.shape, NEG, F32)
    l_ref[...] = jnp.zeros(l_ref.shape, F32)
    acc_ref[...] = jnp.zeros(acc_ref.shape, F32)
    far_bias = bias_ref[n_near - 1, t - 1:t, 0:1]

    def tile(kj, bias, mask):
        rows = pl.ds(pl.multiple_of(kj * t, t), t)
        kt = kb_ref[rows, :]
        vt = vb_ref[rows, :]
        for c in range(2):
            s = lax.dot_general(q[:, c * hd:(c + 1) * hd], kt[:, c * hd:(c + 1) * hd],
                                (((1,), (1,)), ((), ())), preferred_element_type=F32)
            s = s * scale + bias
            if mask is not None:
                s = jnp.where(mask, s, NEG)
            m_old = m_ref[c]
            m_new = jnp.maximum(m_old, jnp.max(s, axis=-1, keepdims=True))
            alpha = jnp.exp(m_old - m_new)
            p = jnp.exp(s - m_new)
            l_ref[c] = alpha * l_ref[c] + jnp.sum(p, axis=-1, keepdims=True)
            acc_ref[c] = alpha * acc_ref[c] + jnp.dot(p.astype(BF16), vt, preferred_element_type=F32)
            m_ref[c] = m_new

    def far_body(kj, carry):
        tile(kj, far_bias, None)
        return carry

    lax.fori_loop(0, jnp.maximum(qi - (n_near - 1), 0), far_body, 0)
    for off in range(n_near - 1, 0, -1):
        @pl.when(qi >= off)
        def _(off=off):
            tile(qi - off, bias_ref[off], None)
    ri = lax.broadcasted_iota(jnp.int32, (t, t), 0)
    ci = lax.broadcasted_iota(jnp.int32, (t, t), 1)
    tile(qi, bias_ref[0], ri >= ci)

    lam = _lambda(lam_ref, lam_init)
    o = acc_ref[0] / l_ref[0] - lam * (acc_ref[1] / l_ref[1])
    y = o * lax.rsqrt(jnp.mean(o * o, axis=-1, keepdims=True) + LN_EPS) * ng_ref[...]
    o_ref[...] = (y * (1.0 - lam_init)).astype(BF16)


def _flash_diff(q, k, v, rel_bias, lam_params, norm_g, n_seq, lam_init):
    m, d = q.shape
    t_seq = m // n_seq
    hw = d // DIFF_HEADS
    hd = hw // 2
    t = min(256, t_seq)
    nq = t_seq // t
    n_near = min(nq, -(-(_first_const_dist(t_seq) - 1) // t) + 1)
    qpos = (jnp.arange(n_near)[:, None] * t + jnp.arange(t)[None, :]).reshape(-1)
    qpos = jnp.tile(qpos, DIFF_HEADS)
    rb_rows = jnp.repeat(rel_bias.T, n_near * t, axis=0)
    bias = _bias_table(qpos, jnp.arange(t), rb_rows, t, t).reshape(DIFF_HEADS, n_near, t, t)
    return pl.pallas_call(
        functools.partial(_flash_kernel, t=t, hd=hd, n_near=n_near, lam_init=lam_init, scale=hd ** -0.5),
        grid=(n_seq, DIFF_HEADS, nq),
        in_specs=[pl.BlockSpec((t, hw), lambda b, h, i: (b * nq + i, h)),
                  pl.BlockSpec((t_seq, hw), lambda b, h, i: (b, h)),
                  pl.BlockSpec((t_seq, hw), lambda b, h, i: (b, h)),
                  pl.BlockSpec((None, n_near, t, t), lambda b, h, i: (h, 0, 0, 0)),
                  pl.BlockSpec((4, hd), lambda b, h, i: (0, 0)),
                  pl.BlockSpec((1, hw), lambda b, h, i: (0, 0))],
        out_specs=pl.BlockSpec((t, hw), lambda b, h, i: (b * nq + i, h)),
        out_shape=jax.ShapeDtypeStruct((m, d), BF16),
        scratch_shapes=[pltpu.VMEM((t_seq, hw), BF16), pltpu.VMEM((t_seq, hw), BF16),
                        pltpu.VMEM((2, t, 1), F32), pltpu.VMEM((2, t, 1), F32),
                        pltpu.VMEM((2, t, hw), F32)],
        compiler_params=_params(("arbitrary", "arbitrary", "arbitrary"), 40),
        name="flash_diff",
    )(q, k, v, bias, lam_params, norm_g.reshape(1, hw))


def _paged_kernel(pt_ref, wq_ref, kn_ref, vn_ref, bias_ref, biasn_ref, lam_ref, ng_ref, *rest,
                  pages, hw, lam_init, scale):
    k_refs = rest[:pages]
    v_refs = rest[pages:2 * pages]
    o_ref, m_ref, l_ref, acc_ref = rest[2 * pages:]
    s_id = pl.program_id(1)
    rows = wq_ref.shape[0]
    half = rows // 2
    wq = wq_ref[...]

    def diag_blocks(full):
        parts = [full[g * SAMPLE_Q_ROWS:(g + 1) * SAMPLE_Q_ROWS,
                      (g % DIFF_HEADS) * hw:(g % DIFF_HEADS + 1) * hw]
                 for g in range(rows // SAMPLE_Q_ROWS)]
        return jnp.concatenate(parts, axis=0)

    def update(s, vmat):
        m_old = m_ref[...]
        m_new = jnp.maximum(m_old, jnp.max(s, axis=-1, keepdims=True))
        alpha = jnp.exp(m_old - m_new)
        p = jnp.exp(s - m_new)
        l_ref[...] = alpha * l_ref[...] + jnp.sum(p, axis=-1, keepdims=True)
        pv = jnp.dot(p.astype(BF16), vmat, preferred_element_type=F32)
        acc_ref[...] = alpha * acc_ref[...] + diag_blocks(pv)
        m_ref[...] = m_new

    @pl.when(s_id == 0)
    def _():
        m_ref[...] = jnp.full(m_ref.shape, NEG, F32)
        l_ref[...] = jnp.zeros(l_ref.shape, F32)
        acc_ref[...] = jnp.zeros(acc_ref.shape, F32)
        kn = kn_ref[...].astype(BF16)
        s = lax.dot_general(wq, kn, (((1,), (1,)), ((), ())), preferred_element_type=F32)
        s = s * scale + biasn_ref[...]
        tq = lax.broadcasted_iota(jnp.int32, s.shape, 0) % SAMPLE_Q_ROWS
        tk = lax.broadcasted_iota(jnp.int32, s.shape, 1)
        update(jnp.where(tk <= tq, s, NEG), vn_ref[...].astype(BF16))

    kp = jnp.concatenate([r[...].astype(BF16) for r in k_refs], axis=0)
    vp = jnp.concatenate([r[...].astype(BF16) for r in v_refs], axis=0)
    s = lax.dot_general(wq, kp, (((1,), (1,)), ((), ())), preferred_element_type=F32)
    update(s * scale + bias_ref[...], vp)

    @pl.when(s_id == pl.num_programs(1) - 1)
    def _():
        a = acc_ref[...] / l_ref[...]
        lam = _lambda(lam_ref, lam_init)
        o = a[:half] - lam * a[half:]
        y = o * lax.rsqrt(jnp.mean(o * o, axis=-1, keepdims=True) + LN_EPS) * ng_ref[...]
        o_ref[...] = (y * (1.0 - lam_init)).astype(o_ref.dtype)


def _paged_diff(q, k_new, v_new, cache_k, cache_v, page_table, layer, rel_bias, lam_params, norm_g,
                n_tok, lam_init):
    nb, tq, d = q.shape
    hw = d // DIFF_HEADS
    hd = hw // 2
    n_pages = page_table.shape[1]
    past = n_pages * PAGE_SIZE
    pages = 2 if n_pages % 2 == 0 else 1
    rows = 2 * DIFF_HEADS * tq
    q5 = q.reshape(nb, tq, DIFF_HEADS, 2, hd).transpose(0, 3, 2, 1, 4)
    same_c = jnp.arange(2)[:, None, None, None] == jnp.arange(2)[None, None, None, :]
    same_h = jnp.arange(DIFF_HEADS)[None, :, None, None] == jnp.arange(DIFF_HEADS)[None, None, :, None]
    sel = same_c & same_h
    wq = jnp.where(sel[None, :, :, None, :, :, None], q5[:, :, :, :, None, None, :], 0.0)
    wq = wq.reshape(nb, rows, d).astype(BF16)
    pad = ((0, 0), (0, PAGE_SIZE - tq), (0, 0))
    k_pad = jnp.pad(k_new, pad)
    v_pad = jnp.pad(v_new, pad)
    qpos = jnp.tile(past + jnp.arange(tq), 2 * DIFF_HEADS)
    rb_rows = jnp.tile(jnp.repeat(rel_bias.T, tq, axis=0), (2, 1))
    bias_past = _bias_table(qpos, jnp.arange(past), rb_rows, rows, math.gcd(past, 2048))
    bias_new = _bias_table(qpos, past + jnp.arange(PAGE_SIZE), rb_rows, rows, PAGE_SIZE)
    ck = cache_k.reshape(cache_k.shape[0], cache_k.shape[1], PAGE_SIZE, d)
    cv = cache_v.reshape(cache_v.shape[0], cache_v.shape[1], PAGE_SIZE, d)

    def page_spec(i):
        return pl.BlockSpec((None, None, PAGE_SIZE, d),
                            lambda b, s, pt: (layer, pt[b, s * pages + i], 0, 0))

    grid_spec = pltpu.PrefetchScalarGridSpec(
        num_scalar_prefetch=1,
        grid=(nb, n_pages // pages),
        in_specs=[pl.BlockSpec((None, rows, d), lambda b, s, pt: (b, 0, 0)),
                  pl.BlockSpec((None, PAGE_SIZE, d), lambda b, s, pt: (b, 0, 0)),
                  pl.BlockSpec((None, PAGE_SIZE, d), lambda b, s, pt: (b, 0, 0)),
                  pl.BlockSpec((rows, pages * PAGE_SIZE), lambda b, s, pt: (0, s)),
                  pl.BlockSpec((rows, PAGE_SIZE), lambda b, s, pt: (0, 0)),
                  pl.BlockSpec((4, hd), lambda b, s, pt: (0, 0)),
                  pl.BlockSpec((1, hw), lambda b, s, pt: (0, 0))]
                 + [page_spec(i) for i in range(pages)] + [page_spec(i) for i in range(pages)],
        out_specs=pl.BlockSpec((None, rows // 2, hw), lambda b, s, pt: (b, 0, 0)),
        scratch_shapes=[pltpu.VMEM((rows, 1), F32), pltpu.VMEM((rows, 1), F32),
                        pltpu.VMEM((rows, hw), F32)],
    )
    out = pl.pallas_call(
        functools.partial(_paged_kernel, pages=pages, hw=hw, lam_init=lam_init, scale=hd ** -0.5),
        grid_spec=grid_spec,
        out_shape=jax.ShapeDtypeStruct((nb, rows // 2, hw), BF16),
        compiler_params=_params(("arbitrary", "arbitrary"), 56),
        name="paged_diff",
    )(page_table, wq, k_pad, v_pad, bias_past, bias_new, lam_params, norm_g.reshape(1, hw),
      *([ck] * pages), *([cv] * pages))
    return out.reshape(nb, DIFF_HEADS, tq, hw).transpose(0, 2, 1, 3).reshape(nb, tq, d)


def _run(x, n_seq, valid, gla_s0, paged, prm):
    (ln_g, ln_b, ffn_w_gate, ffn_w_up, ffn_w_down, gla_w_in, gla_w_a2, gla_b_a, gla_norm_g, gla_w_o,
     diff_w_in, diff_lambda, diff_norm_g, diff_w_o, rel_bias) = prm
    m, d = x.shape
    t = m // n_seq
    xf, xb = x, _to_bf16(x)
    gla_states, k_rows, v_rows = [], [], []

    def ffn(xf, xb, i, s):
        h = _ffn_up(xb, ffn_w_gate, ffn_w_up, (i, s))
        z = _res_mm(h, ffn_w_down, (i, s), xf, 0.5)
        return _layer_norm(z, ln_g[i, 2 * s], ln_b[i, 2 * s])

    for i in range(DEPTH):
        xf, xb = ffn(xf, xb, i, 0)
        j = i // N_MIXERS
        if i % N_MIXERS == 0:
            hk = gla_w_a2.shape[-1]
            hv = gla_w_o.shape[1]
            qkvr = _proj(xb, gla_w_in, (j,), 0, 2 * hk + 2 * hv, 512)
            w_g = jnp.pad(gla_w_in[j][:, 2 * hk + 2 * hv:], ((0, 0), (0, LANES - GLA_GATE_RANK)))
            glr = _proj(xb, w_g[None], (0,), 0, LANES, LANES)
            w_a2p = jnp.pad(gla_w_a2[j], ((0, LANES - GLA_GATE_RANK), (0, 0)))
            s0 = None if gla_s0 is None else gla_s0[j]
            o, s_fin = _gla_core(qkvr, glr, w_a2p, gla_b_a[j], gla_norm_g[j], s0, n_seq, valid)
            gla_states.append(s_fin)
            z = _res_mm(o, gla_w_o, (j,), xf, 1.0)
        else:
            lam_init = 0.8 - 0.6 * math.exp(-0.3 * i)
            q = _proj(xb, diff_w_in, (j,), 0, d, 512)
            k = _proj(xb, diff_w_in, (j,), d // 512, d, 512)
            v = _proj(xb, diff_w_in, (j,), 2 * d // 512, d, 512)
            k_rows.append(k)
            v_rows.append(v)
            if paged is None:
                o = _flash_diff(q, k, v, rel_bias, diff_lambda[j], diff_norm_g[j], n_seq, lam_init)
            else:
                cache_k, cache_v, page_table = paged
                first = lambda a: a.reshape(n_seq, t, d)[:, :SAMPLE_Q_ROWS]
                o = _paged_diff(first(q), first(k), first(v), cache_k, cache_v, page_table, j, rel_bias,
                                diff_lambda[j], diff_norm_g[j], valid, lam_init)
                o = jnp.pad(o, ((0, 0), (0, t - SAMPLE_Q_ROWS), (0, 0))).reshape(m, d)
            z = _res_mm(o, diff_w_o, (j,), xf, 1.0)
        xf, xb = _layer_norm(z, ln_g[i, 1], ln_b[i, 1])
        xf, xb = ffn(xf, xb, i, 1)
    return xf, jnp.stack(gla_states), jnp.stack(k_rows), jnp.stack(v_rows)


def kernel(x_prompt, x_sample, state_gla, cache_k, cache_v, page_table, ln_g, ln_b, ffn_w_gate, ffn_w_up,
           ffn_w_down, gla_w_in, gla_w_a2, gla_b_a, gla_norm_g, gla_w_o, diff_w_in, diff_lambda, diff_norm_g,
           diff_w_o, rel_bias):
    prm = (ln_g, ln_b, ffn_w_gate, ffn_w_up, ffn_w_down, gla_w_in, gla_w_a2, gla_b_a, gla_norm_g, gla_w_o,
           diff_w_in, diff_lambda, diff_norm_g, diff_w_o, rel_bias)
    nb, t, d = x_prompt.shape
    hw = d // DIFF_HEADS
    y, gs, kr, vr = _run(x_prompt.reshape(nb * t, d), nb, t, None, None, prm)
    y_prompt = y.reshape(nb, t, d)
    k_prompt = kr.reshape(-1, nb, t, DIFF_HEADS, hw)
    v_prompt = vr.reshape(-1, nb, t, DIFF_HEADS, hw)

    sb, st, _ = x_sample.shape
    xs = jnp.pad(x_sample, ((0, 0), (0, SAMPLE_ROWS - st), (0, 0))).reshape(sb * SAMPLE_ROWS, d)
    ys, gss, krs, vrs = _run(xs, sb, st, state_gla, (cache_k, cache_v, page_table), prm)
    unpad = lambda a: a.reshape(a.shape[:-2] + (sb, SAMPLE_ROWS, d))[..., :st, :]
    y_sample = unpad(ys)
    k_sample = unpad(krs).reshape(-1, sb, st, DIFF_HEADS, hw)
    v_sample = unpad(vrs).reshape(-1, sb, st, DIFF_HEADS, hw)
    return (y_prompt, y_sample, gs, gss, k_prompt, v_prompt, k_sample, v_sample)
```

```python
import functools
import math

import jax
import jax.numpy as jnp
import numpy as np
from jax import lax
from jax.experimental import pallas as pl
from jax.experimental.pallas import tpu as pltpu

F32 = jnp.float32
BF16 = jnp.bfloat16

DEPTH = 2
N_MIXERS = 2
GLA_HEADS = 4
GLA_GATE_RANK = 16
GLA_TAU = 16.0
GLA_CHUNK = 64
DIFF_HEADS = 16
PAGE_SIZE = 128
REL_BUCKETS = 32
REL_MAX_DIST = 128
DEEPNORM_ALPHA = (2.0 * DEPTH) ** 0.25
LN_EPS = 1e-5
NEG = -1e30

LANES = 128
MIB = 1024 * 1024
SPLIT_K_ABOVE = 8192
FLASH_TILE = 512
SAMPLE_ROWS = 64
SAMPLE_KEY_ROWS = 8


def _row_tile(m, cap):
    for t in range(min(m, cap) // 16 * 16, 15, -16):
        if m % t == 0:
            return t
    raise ValueError(f"no row tile for {m} rows")


def _params(sem, vmem_mib):
    return pltpu.CompilerParams(dimension_semantics=sem, vmem_limit_bytes=vmem_mib * MIB)


def _ln_kernel(z_ref, g_ref, b_ref, xf_ref, xb_ref):
    z = z_ref[...]
    zc = z - jnp.mean(z, axis=-1, keepdims=True)
    var = jnp.mean(zc * zc, axis=-1, keepdims=True)
    y = zc * lax.rsqrt(var + LN_EPS) * g_ref[...] + b_ref[...]
    xf_ref[...] = y
    xb_ref[...] = y.astype(BF16)


def _layer_norm(z, g, b):
    m, d = z.shape
    tr = _row_tile(m, 256)
    return pl.pallas_call(
        _ln_kernel,
        grid=(m // tr,),
        in_specs=[pl.BlockSpec((tr, d), lambda i: (i, 0)),
                  pl.BlockSpec((1, d), lambda i: (0, 0)),
                  pl.BlockSpec((1, d), lambda i: (0, 0))],
        out_specs=[pl.BlockSpec((tr, d), lambda i: (i, 0)),
                   pl.BlockSpec((tr, d), lambda i: (i, 0))],
        out_shape=[jax.ShapeDtypeStruct((m, d), F32), jax.ShapeDtypeStruct((m, d), BF16)],
        compiler_params=_params(("arbitrary",), 40),
        name="layer_norm",
    )(z, g.reshape(1, d), b.reshape(1, d))


def _cast_kernel(x_ref, o_ref):
    o_ref[...] = x_ref[...].astype(BF16)


def _to_bf16(x):
    m, d = x.shape
    tr = _row_tile(m, 256)
    return pl.pallas_call(
        _cast_kernel,
        grid=(m // tr,),
        in_specs=[pl.BlockSpec((tr, d), lambda i: (i, 0))],
        out_specs=pl.BlockSpec((tr, d), lambda i: (i, 0)),
        out_shape=jax.ShapeDtypeStruct((m, d), BF16),
        compiler_params=_params(("arbitrary",), 40),
        name="to_bf16",
    )(x)


def _w_spec(w, lead, k, tn, col0):
    none = (None,) * len(lead)
    return pl.BlockSpec(none + (k, tn), lambda n, m: tuple(lead) + (0, col0 + n))


def _proj_kernel(x_ref, w_ref, o_ref, wb_ref):
    @pl.when(pl.program_id(1) == 0)
    def _():
        wb_ref[...] = w_ref[...].astype(BF16)

    o_ref[...] = jnp.dot(x_ref[...], wb_ref[...], preferred_element_type=F32).astype(o_ref.dtype)


def _proj(x_bf, w, lead, col0, n_cols, tn, out_dtype=F32):
    m, k = x_bf.shape
    tm = _row_tile(m, 1024)
    return pl.pallas_call(
        _proj_kernel,
        grid=(n_cols // tn, m // tm),
        in_specs=[pl.BlockSpec((tm, k), lambda n, i: (i, 0)), _w_spec(w, lead, k, tn, col0)],
        out_specs=pl.BlockSpec((tm, tn), lambda n, i: (i, n)),
        out_shape=jax.ShapeDtypeStruct((m, n_cols), out_dtype),
        scratch_shapes=[pltpu.VMEM((k, tn), BF16)],
        compiler_params=_params(("arbitrary", "arbitrary"), 56),
        name="proj",
    )(x_bf, w)


def _ffn_up_kernel(x_ref, wg_ref, wu_ref, h_ref, wgb_ref, wub_ref):
    @pl.when(pl.program_id(1) == 0)
    def _():
        wgb_ref[...] = wg_ref[...].astype(BF16)
        wub_ref[...] = wu_ref[...].astype(BF16)

    x = x_ref[...]
    g = jnp.dot(x, wgb_ref[...], preferred_element_type=F32)
    u = jnp.dot(x, wub_ref[...], preferred_element_type=F32)
    h_ref[...] = (g * jax.nn.sigmoid(g) * u).astype(BF16)


def _ffn_up(x_bf, w_gate, w_up, lead):
    m, k = x_bf.shape
    f = w_gate.shape[-1]
    tm = _row_tile(m, 1024)
    tn = 256
    return pl.pallas_call(
        _ffn_up_kernel,
        grid=(f // tn, m // tm),
        in_specs=[pl.BlockSpec((tm, k), lambda n, i: (i, 0)),
                  _w_spec(w_gate, lead, k, tn, 0), _w_spec(w_up, lead, k, tn, 0)],
        out_specs=pl.BlockSpec((tm, tn), lambda n, i: (i, n)),
        out_shape=jax.ShapeDtypeStruct((m, f), BF16),
        scratch_shapes=[pltpu.VMEM((k, tn), BF16), pltpu.VMEM((k, tn), BF16)],
        compiler_params=_params(("arbitrary", "arbitrary"), 56),
        name="ffn_up",
    )(x_bf, w_gate, w_up)


def _res_mm_kernel(*refs, alpha, scale, has_partial, final):
    h_ref, w_ref = refs[:2]
    o_ref, wb_ref = refs[-2:]

    @pl.when(pl.program_id(1) == 0)
    def _():
        wb_ref[...] = w_ref[...].astype(BF16)

    y = jnp.dot(h_ref[...], wb_ref[...], preferred_element_type=F32)
    if has_partial:
        y = refs[2][...] + y
    if final:
        y = alpha * refs[-3][...] + scale * y
    o_ref[...] = y


def _res_mm_pass(h_bf, w, lead, kb, n_kb, partial, res, scale, tm_cap):
    m, k = h_bf.shape
    kh = k // n_kb
    n_cols = w.shape[-1]
    tn = 512
    tm = _row_tile(m, tm_cap)
    none = (None,) * len(lead)
    tile = pl.BlockSpec((tm, tn), lambda n, i: (i, n))
    in_specs = [pl.BlockSpec((tm, kh), lambda n, i: (i, kb)),
                pl.BlockSpec(none + (kh, tn), lambda n, i: tuple(lead) + (kb, n))]
    args = [h_bf, w]
    for extra in (partial, res):
        if extra is not None:
            in_specs.append(tile)
            args.append(extra)
    return pl.pallas_call(
        functools.partial(_res_mm_kernel, alpha=DEEPNORM_ALPHA, scale=scale,
                          has_partial=partial is not None, final=res is not None),
        grid=(n_cols // tn, m // tm),
        in_specs=in_specs,
        out_specs=tile,
        out_shape=jax.ShapeDtypeStruct((m, n_cols), F32),
        scratch_shapes=[pltpu.VMEM((kh, tn), BF16)],
        compiler_params=_params(("arbitrary", "arbitrary"), 56),
        name="res_mm",
    )(*args)


def _res_mm(h_bf, w, lead, res, scale):
    k = h_bf.shape[1]
    if k > SPLIT_K_ABOVE and (k // 2) % LANES == 0:
        part = _res_mm_pass(h_bf, w, lead, 0, 2, None, None, scale, 512)
        return _res_mm_pass(h_bf, w, lead, 1, 2, part, res, scale, 512)
    return _res_mm_pass(h_bf, w, lead, 0, 1, None, res, scale, 1024)


def _log_sigmoid(x):
    return jnp.minimum(x, 0.0) - jnp.log(1.0 + jnp.exp(-jnp.abs(x)))


def _gla_kernel(*refs, chunk, valid, has_s0, q_scale):
    if has_s0:
        (q_ref, k_ref, v_ref, r_ref, glr_ref, wa2_ref, ba_ref, ng_ref, s0_ref,
         o_ref, sfin_ref, s_ref) = refs
    else:
        (q_ref, k_ref, v_ref, r_ref, glr_ref, wa2_ref, ba_ref, ng_ref,
         o_ref, sfin_ref, s_ref) = refs
    c = pl.program_id(2)

    @pl.when(c == 0)
    def _():
        if has_s0:
            s_ref[...] = s0_ref[...]
        else:
            s_ref[...] = jnp.zeros_like(s_ref)

    dk = q_ref.shape[-1]
    pre = jnp.dot(glr_ref[...].astype(BF16), wa2_ref[...].astype(BF16),
                  preferred_element_type=F32) + ba_ref[...]
    log_a = _log_sigmoid(pre) / GLA_TAU
    k = k_ref[...]
    if valid < chunk:
        row_ok = lax.broadcasted_iota(jnp.int32, (chunk, 1), 0) < valid
        log_a = jnp.where(row_ok, log_a, 0.0)
        k = jnp.where(row_ok, k, 0.0)
    ri = lax.broadcasted_iota(jnp.int32, (chunk, chunk), 0)
    ci = lax.broadcasted_iota(jnp.int32, (chunk, chunk), 1)
    causal = ri >= ci
    b = jnp.dot(jnp.where(causal, 1.0, 0.0), log_a, precision=lax.Precision.HIGHEST,
                preferred_element_type=F32)
    b_last = b[chunk - 1:chunk, :]
    q_t = (q_ref[...] * q_scale * jnp.exp(b)).astype(BF16)
    k_t = (k * jnp.exp(-b)).astype(BF16)
    k_end = (k * jnp.exp(b_last - b)).astype(BF16)
    v = v_ref[...].astype(BF16)
    att = lax.dot_general(q_t, k_t, (((1,), (1,)), ((), ())), preferred_element_type=F32)
    att = jnp.where(causal, att, 0.0).astype(BF16)
    s_old = s_ref[...]
    o = (jnp.dot(att, v, preferred_element_type=F32)
         + jnp.dot(q_t, s_old.astype(BF16), preferred_element_type=F32))
    kv = lax.dot_general(k_end, v, (((0,), (0,)), ((), ())), preferred_element_type=F32)
    decay_col = jnp.exp(jnp.broadcast_to(b_last, (LANES, dk)).T[:, :1])
    s_new = decay_col * s_old + kv
    s_ref[...] = s_new

    @pl.when(c == pl.num_programs(2) - 1)
    def _():
        sfin_ref[...] = s_new

    oc = o - jnp.mean(o, axis=-1, keepdims=True)
    var = jnp.mean(oc * oc, axis=-1, keepdims=True)
    on = oc * lax.rsqrt(var + LN_EPS) * ng_ref[...]
    r = r_ref[...]
    o_ref[...] = (on * (r * jax.nn.sigmoid(r))).astype(BF16)


def _gla_core(qkvr, glr, w_a2p, b_a, norm_g, s0, n_seq, valid):
    m = qkvr.shape[0]
    hk = w_a2p.shape[-1]
    dk = hk // GLA_HEADS
    dv = norm_g.shape[-1]
    t = m // n_seq
    chunk = GLA_CHUNK if t % GLA_CHUNK == 0 else t
    nc = t // chunk
    kb0 = hk // dk
    vb0 = 2 * hk // dv
    rb0 = vb0 + GLA_HEADS
    has_s0 = s0 is not None
    row = lambda b, h, c: b * nc + c
    in_specs = [
        pl.BlockSpec((chunk, dk), lambda b, h, c: (row(b, h, c), h)),
        pl.BlockSpec((chunk, dk), lambda b, h, c: (row(b, h, c), kb0 + h)),
        pl.BlockSpec((chunk, dv), lambda b, h, c: (row(b, h, c), vb0 + h)),
        pl.BlockSpec((chunk, dv), lambda b, h, c: (row(b, h, c), rb0 + h)),
        pl.BlockSpec((chunk, LANES), lambda b, h, c: (row(b, h, c), 0)),
        pl.BlockSpec((LANES, dk), lambda b, h, c: (0, h)),
        pl.BlockSpec((1, dk), lambda b, h, c: (0, h)),
        pl.BlockSpec((1, dv), lambda b, h, c: (0, 0)),
    ]
    args = [qkvr, qkvr, qkvr, qkvr, glr, w_a2p, b_a.reshape(1, hk), norm_g.reshape(1, dv)]
    if has_s0:
        in_specs.append(pl.BlockSpec((None, None, dk, dv), lambda b, h, c: (b, h, 0, 0)))
        args.append(s0)
    return pl.pallas_call(
        functools.partial(_gla_kernel, chunk=chunk, valid=valid, has_s0=has_s0, q_scale=dk ** -0.5),
        grid=(n_seq, GLA_HEADS, nc),
        in_specs=in_specs,
        out_specs=[pl.BlockSpec((chunk, dv), lambda b, h, c: (row(b, h, c), h)),
                   pl.BlockSpec((None, None, dk, dv), lambda b, h, c: (b, h, 0, 0))],
        out_shape=[jax.ShapeDtypeStruct((m, GLA_HEADS * dv), BF16),
                   jax.ShapeDtypeStruct((n_seq, GLA_HEADS, dk, dv), F32)],
        scratch_shapes=[pltpu.VMEM((dk, dv), F32)],
        compiler_params=_params(("arbitrary", "arbitrary", "arbitrary"), 48),
        name="gla_core",
    )(*args)


def _t5_bucket(dist):
    n = jnp.maximum(dist, 0)
    max_exact = REL_BUCKETS // 2
    large = max_exact + (jnp.log(jnp.maximum(n, 1).astype(F32) / max_exact)
                         / math.log(REL_MAX_DIST / max_exact) * (REL_BUCKETS - max_exact)).astype(jnp.int32)
    large = jnp.minimum(large, REL_BUCKETS - 1)
    return jnp.where(n < max_exact, n, large)


def _bias_kernel(qpos_ref, kpos_ref, qgrp_ref, kgrp_ref, rb_ref, o_ref):
    bucket = _t5_bucket(qpos_ref[...] - kpos_ref[...])
    rb = rb_ref[...]
    acc = jnp.zeros(o_ref.shape, F32)
    for i in range(REL_BUCKETS):
        acc = jnp.where(bucket == i, rb[:, i:i + 1], acc)
    o_ref[...] = jnp.where(qgrp_ref[...] == kgrp_ref[...], acc, NEG)


def _bias_table(qpos, kpos, rb_rows, tr, tc, qgrp=None, kgrp=None):
    r, c = qpos.shape[0], kpos.shape[0]
    qgrp = jnp.zeros((r,), jnp.int32) if qgrp is None else qgrp
    kgrp = jnp.zeros((c,), jnp.int32) if kgrp is None else kgrp
    col = lambda a: a.reshape(r, 1).astype(jnp.int32)
    row = lambda a: a.reshape(1, c).astype(jnp.int32)
    return pl.pallas_call(
        _bias_kernel,
        grid=(r // tr, c // tc),
        in_specs=[pl.BlockSpec((tr, 1), lambda i, j: (i, 0)),
                  pl.BlockSpec((1, tc), lambda i, j: (0, j)),
                  pl.BlockSpec((tr, 1), lambda i, j: (i, 0)),
                  pl.BlockSpec((1, tc), lambda i, j: (0, j)),
                  pl.BlockSpec((tr, REL_BUCKETS), lambda i, j: (i, 0))],
        out_specs=pl.BlockSpec((tr, tc), lambda i, j: (i, j)),
        out_shape=jax.ShapeDtypeStruct((r, c), F32),
        compiler_params=_params(("arbitrary", "arbitrary"), 32),
        name="t5_bias",
    )(col(qpos), row(kpos), col(qgrp), row(kgrp), rb_rows)


def _first_const_dist(limit):
    n = np.arange(1, limit + 1)
    large = 16 + (np.log(n.astype(np.float32) / np.float32(16)) / np.float32(math.log(8.0))
                  * np.float32(16)).astype(np.int32)
    bucket = np.where(n < 16, n, np.minimum(large, REL_BUCKETS - 1))
    not_last = np.nonzero(bucket != REL_BUCKETS - 1)[0]
    return int(n[not_last[-1]] + 1) if not_last.size else 1


def _lambda(lam_ref, lam_init):
    lp = lam_ref[...]
    a = jnp.sum(lp[0:1] * lp[1:2], axis=-1, keepdims=True)
    b = jnp.sum(lp[2:3] * lp[3:4], axis=-1, keepdims=True)
    return jnp.exp(a) - jnp.exp(b) + lam_init


def _flash_kernel(q_ref, k_ref, v_ref, bias_ref, lam_ref, ng_ref, o_ref,
                  kb_ref, vb_ref, m_ref, l_ref, acc_ref, *, t, hd, n_near, lam_init, scale):
    qi = pl.program_id(2)

    @pl.when(qi == 0)
    def _():
        kb_ref[...] = k_ref[...].astype(BF16)
        vb_ref[...] = v_ref[...].astype(BF16)

    q = q_ref[...]
    lane = lax.broadcasted_iota(jnp.int32, q.shape, 1)
    q2 = jnp.concatenate([jnp.where(lane < hd, q, 0.0), jnp.where(lane >= hd, q, 0.0)],
                         axis=0).astype(BF16)
    m_ref[...] = jnp.full(m_ref.shape, NEG, F32)
    l_ref[...] = jnp.zeros(l_ref.shape, F32)
    acc_ref[...] = jnp.zeros(acc_ref.shape, F32)
    far_bias = bias_ref[n_near - 1, t - 1:t, 0:1]

    def tile(kj, bias, mask):
        rows = pl.ds(pl.multiple_of(kj * t, t), t)
        s = lax.dot_general(q2, kb_ref[rows, :], (((1,), (1,)), ((), ())),
                            preferred_element_type=F32)
        s = s * scale + bias
        if mask is not None:
            s = jnp.where(mask, s, NEG)
        m_old = m_ref[...]
        m_new = jnp.maximum(m_old, jnp.max(s, axis=-1, keepdims=True))
        alpha = jnp.exp(m_old - m_new)
        p = jnp.exp(s - pltpu.repeat(m_new, t // LANES, axis=1))
        l_ref[...] = alpha * l_ref[...] + jnp.sum(p, axis=-1, keepdims=True)
        acc_ref[...] = (pltpu.repeat(alpha, 2 * hd // LANES, axis=1) * acc_ref[...]
                        + jnp.dot(p.astype(BF16), vb_ref[rows, :], preferred_element_type=F32))
        m_ref[...] = m_new

    def far_body(kj, carry):
        tile(kj, far_bias, None)
        return carry

    lax.fori_loop(0, jnp.maximum(qi - (n_near - 1), 0), far_body, 0)
    for off in range(n_near - 1, 0, -1):
        @pl.when(qi >= off)
        def _(off=off):
            b = bias_ref[off]
            tile(qi - off, jnp.concatenate([b, b], axis=0), None)
    ri = lax.broadcasted_iota(jnp.int32, (2 * t, t), 0)
    ci = lax.broadcasted_iota(jnp.int32, (2 * t, t), 1)
    b = bias_ref[0]
    tile(qi, jnp.concatenate([b, b], axis=0), jnp.where(ri >= t, ri - t, ri) >= ci)

    lam = _lambda(lam_ref, lam_init)
    a = acc_ref[...] / pltpu.repeat(l_ref[...], 2 * hd // LANES, axis=1)
    o = a[:t] - lam * a[t:]
    y = o * lax.rsqrt(jnp.mean(o * o, axis=-1, keepdims=True) + LN_EPS) * ng_ref[...]
    o_ref[...] = (y * (1.0 - lam_init)).astype(BF16)


def _flash_diff(q, k, v, rel_bias, lam_params, norm_g, n_seq, lam_init):
    m, d = q.shape
    t_seq = m // n_seq
    hw = d // DIFF_HEADS
    hd = hw // 2
    t = next(c for c in range(FLASH_TILE, 0, -LANES) if t_seq % c == 0)
    nq = t_seq // t
    n_near = min(nq, -(-(_first_const_dist(t_seq) - 1) // t) + 1)
    qpos = (jnp.arange(n_near)[:, None] * t + jnp.arange(t)[None, :]).reshape(-1)
    qpos = jnp.tile(qpos, DIFF_HEADS)
    rb_rows = jnp.repeat(rel_bias.T, n_near * t, axis=0)
    bias = _bias_table(qpos, jnp.arange(t), rb_rows, t, t).reshape(DIFF_HEADS, n_near, t, t)
    return pl.pallas_call(
        functools.partial(_flash_kernel, t=t, hd=hd, n_near=n_near, lam_init=lam_init, scale=hd ** -0.5),
        grid=(n_seq, DIFF_HEADS, nq),
        in_specs=[pl.BlockSpec((t, hw), lambda b, h, i: (b * nq + i, h)),
                  pl.BlockSpec((t_seq, hw), lambda b, h, i: (b, h)),
                  pl.BlockSpec((t_seq, hw), lambda b, h, i: (b, h)),
                  pl.BlockSpec((None, n_near, t, t), lambda b, h, i: (h, 0, 0, 0)),
                  pl.BlockSpec((4, hd), lambda b, h, i: (0, 0)),
                  pl.BlockSpec((1, hw), lambda b, h, i: (0, 0))],
        out_specs=pl.BlockSpec((t, hw), lambda b, h, i: (b * nq + i, h)),
        out_shape=jax.ShapeDtypeStruct((m, d), BF16),
        scratch_shapes=[pltpu.VMEM((t_seq, hw), BF16), pltpu.VMEM((t_seq, hw), BF16),
                        pltpu.VMEM((2 * t, LANES), F32), pltpu.VMEM((2 * t, LANES), F32),
                        pltpu.VMEM((2 * t, hw), F32)],
        compiler_params=_params(("arbitrary", "arbitrary", "arbitrary"), 48),
        name="flash_diff",
    )(q, k, v, bias, lam_params, norm_g.reshape(1, hw))


def _paged_kernel(pt_ref, q_ref, kn_ref, vn_ref, tab_ref, tabn_ref, lam_ref, ng_ref, *rest,
                  pages, q_tok, lam_init, scale):
    k_refs = rest[:pages]
    v_refs = rest[pages:2 * pages]
    o_ref, m_ref, l_ref, acc_ref = rest[2 * pages:]
    s_id = pl.program_id(1)
    rows, hw = q_ref.shape
    half = rows // 2
    q = q_ref[...]

    def scores(kmat):
        return lax.dot_general(q, kmat, (((1,), (1,)), ((), ())), preferred_element_type=F32) * scale

    def update(s, vmat):
        m_old = m_ref[...]
        m_new = jnp.maximum(m_old, jnp.max(s, axis=-1, keepdims=True))
        alpha = jnp.exp(m_old - m_new)
        p = jnp.exp(s - pltpu.repeat(m_new, s.shape[1] // LANES, axis=1))
        l_ref[...] = alpha * l_ref[...] + jnp.sum(p, axis=-1, keepdims=True)
        acc_ref[...] = (pltpu.repeat(alpha, hw // LANES, axis=1) * acc_ref[...]
                        + jnp.dot(p.astype(BF16), vmat, preferred_element_type=F32))
        m_ref[...] = m_new

    @pl.when(s_id == 0)
    def _():
        m_ref[...] = jnp.full(m_ref.shape, NEG, F32)
        l_ref[...] = jnp.zeros(l_ref.shape, F32)
        acc_ref[...] = jnp.zeros(acc_ref.shape, F32)
        s = scores(kn_ref[...].astype(BF16)) + tabn_ref[...]
        tq = lax.broadcasted_iota(jnp.int32, s.shape, 0) % q_tok
        tk = lax.broadcasted_iota(jnp.int32, s.shape, 1) // DIFF_HEADS
        update(jnp.where(tk <= tq, s, NEG), vn_ref[...].astype(BF16))

    kp = jnp.concatenate([r[...].astype(BF16) for r in k_refs], axis=0)
    vp = jnp.concatenate([r[...].astype(BF16) for r in v_refs], axis=0)
    update(scores(kp) + tab_ref[...], vp)

    @pl.when(s_id == pl.num_programs(1) - 1)
    def _():
        a = acc_ref[...] / pltpu.repeat(l_ref[...], hw // LANES, axis=1)
        lam = _lambda(lam_ref, lam_init)
        o = a[:half] - lam * a[half:]
        y = o * lax.rsqrt(jnp.mean(o * o, axis=-1, keepdims=True) + LN_EPS) * ng_ref[...]
        o_ref[...] = (y * (1.0 - lam_init)).astype(o_ref.dtype)


def _paged_diff(q, k_new, v_new, cache_k, cache_v, page_table, layer, rel_bias, lam_params, norm_g,
                lam_init):
    nb, tq, d = q.shape
    tk = k_new.shape[1]
    assert tq <= tk
    hw = d // DIFF_HEADS
    hd = hw // 2
    n_pages = page_table.shape[1]
    past = n_pages * PAGE_SIZE
    pages = 2 if n_pages % 2 == 0 else 1
    n_steps = n_pages // pages
    rows = 2 * DIFF_HEADS * tq
    page_rows = PAGE_SIZE * DIFF_HEADS
    assert pages * PAGE_SIZE + 1 >= _first_const_dist(past + tq)
    q5 = q.reshape(nb, tq, DIFF_HEADS, 2, hd).transpose(0, 3, 2, 1, 4)
    same_c = jnp.arange(2)[:, None] == jnp.arange(2)[None, :]
    qd = jnp.where(same_c[None, :, None, None, :, None], q5[:, :, :, :, None, :], 0.0)
    qd = qd.reshape(nb, rows, hw).astype(BF16)
    k_rows = k_new.reshape(nb, tk * DIFF_HEADS, hw)
    v_rows = v_new.reshape(nb, tk * DIFF_HEADS, hw)
    qpos = jnp.tile(past + jnp.arange(tq), 2 * DIFF_HEADS)
    qgrp = jnp.tile(jnp.repeat(jnp.arange(DIFF_HEADS), tq), 2)
    rb_rows = jnp.tile(jnp.repeat(rel_bias.T, tq, axis=0), (2, 1))
    col_tok = jnp.repeat(jnp.arange(pages * PAGE_SIZE), DIFF_HEADS)
    col_grp = jnp.tile(jnp.arange(DIFF_HEADS), pages * PAGE_SIZE)
    kpos = jnp.concatenate([jnp.zeros_like(col_tok), (n_steps - 1) * pages * PAGE_SIZE + col_tok])
    tab = _bias_table(qpos, kpos, rb_rows, rows, page_rows, qgrp, jnp.tile(col_grp, 2))
    tab = tab.reshape(rows, 2, pages * page_rows).transpose(1, 0, 2)
    tab_new = _bias_table(qpos, past + jnp.repeat(jnp.arange(tk), DIFF_HEADS), rb_rows, rows,
                          tk * DIFF_HEADS, qgrp, jnp.tile(jnp.arange(DIFF_HEADS), tk))
    ck = cache_k.reshape(cache_k.shape[0], cache_k.shape[1], page_rows, hw)
    cv = cache_v.reshape(cache_v.shape[0], cache_v.shape[1], page_rows, hw)

    def page_spec(i):
        return pl.BlockSpec((None, None, page_rows, hw),
                            lambda b, s, pt: (layer, pt[b, s * pages + i], 0, 0))

    grid_spec = pltpu.PrefetchScalarGridSpec(
        num_scalar_prefetch=1,
        grid=(nb, n_steps),
        in_specs=[pl.BlockSpec((None, rows, hw), lambda b, s, pt: (b, 0, 0)),
                  pl.BlockSpec((None, tk * DIFF_HEADS, hw), lambda b, s, pt: (b, 0, 0)),
                  pl.BlockSpec((None, tk * DIFF_HEADS, hw), lambda b, s, pt: (b, 0, 0)),
                  pl.BlockSpec((None, rows, pages * page_rows),
                               lambda b, s, pt: (jnp.where(s == n_steps - 1, 1, 0), 0, 0)),
                  pl.BlockSpec((rows, tk * DIFF_HEADS), lambda b, s, pt: (0, 0)),
                  pl.BlockSpec((4, hd), lambda b, s, pt: (0, 0)),
                  pl.BlockSpec((1, hw), lambda b, s, pt: (0, 0))]
                 + [page_spec(i) for i in range(pages)] + [page_spec(i) for i in range(pages)],
        out_specs=pl.BlockSpec((None, rows // 2, hw), lambda b, s, pt: (b, 0, 0)),
        scratch_shapes=[pltpu.VMEM((rows, LANES), F32), pltpu.VMEM((rows, LANES), F32),
                        pltpu.VMEM((rows, hw), F32)],
    )
    out = pl.pallas_call(
        functools.partial(_paged_kernel, pages=pages, q_tok=tq, lam_init=lam_init, scale=hd ** -0.5),
        grid_spec=grid_spec,
        out_shape=jax.ShapeDtypeStruct((nb, rows // 2, hw), BF16),
        compiler_params=_params(("arbitrary", "arbitrary"), 56),
        name="paged_diff",
    )(page_table, qd, k_rows, v_rows, tab, tab_new, lam_params, norm_g.reshape(1, hw),
      *([ck] * pages), *([cv] * pages))
    return out.reshape(nb, DIFF_HEADS, tq, hw).transpose(0, 2, 1, 3).reshape(nb, tq, d)


def _run(x, n_seq, valid, gla_s0, paged, prm):
    (ln_g, ln_b, ffn_w_gate, ffn_w_up, ffn_w_down, gla_w_in, gla_w_a2, gla_b_a, gla_norm_g, gla_w_o,
     diff_w_in, diff_lambda, diff_norm_g, diff_w_o, rel_bias) = prm
    m, d = x.shape
    t = m // n_seq
    xf, xb = x, _to_bf16(x)
    gla_states, k_rows, v_rows = [], [], []

    def ffn(xf, xb, i, s):
        h = _ffn_up(xb, ffn_w_gate, ffn_w_up, (i, s))
        z = _res_mm(h, ffn_w_down, (i, s), xf, 0.5)
        return _layer_norm(z, ln_g[i, 2 * s], ln_b[i, 2 * s])

    for i in range(DEPTH):
        xf, xb = ffn(xf, xb, i, 0)
        j = i // N_MIXERS
        if i % N_MIXERS == 0:
            hk = gla_w_a2.shape[-1]
            hv = gla_w_o.shape[1]
            qkvr = _proj(xb, gla_w_in, (j,), 0, 2 * hk + 2 * hv, 512)
            w_g = jnp.pad(gla_w_in[j][:, 2 * hk + 2 * hv:], ((0, 0), (0, LANES - GLA_GATE_RANK)))
            glr = _proj(xb, w_g[None], (0,), 0, LANES, LANES)
            w_a2p = jnp.pad(gla_w_a2[j], ((0, LANES - GLA_GATE_RANK), (0, 0)))
            s0 = None if gla_s0 is None else gla_s0[j]
            o, s_fin = _gla_core(qkvr, glr, w_a2p, gla_b_a[j], gla_norm_g[j], s0, n_seq, valid)
            gla_states.append(s_fin)
            z = _res_mm(o, gla_w_o, (j,), xf, 1.0)
        else:
            lam_init = 0.8 - 0.6 * math.exp(-0.3 * i)
            q = _proj(xb, diff_w_in, (j,), 0, d, 512)
            k = _proj(xb, diff_w_in, (j,), d // 512, d, 512)
            v = _proj(xb, diff_w_in, (j,), 2 * d // 512, d, 512)
            k_rows.append(k)
            v_rows.append(v)
            if paged is None:
                o = _flash_diff(q, k, v, rel_bias, diff_lambda[j], diff_norm_g[j], n_seq, lam_init)
            else:
                cache_k, cache_v, page_table = paged
                first = lambda a, n: a.reshape(n_seq, t, d)[:, :n]
                o = _paged_diff(first(q, valid), first(k, SAMPLE_KEY_ROWS), first(v, SAMPLE_KEY_ROWS),
                                cache_k, cache_v, page_table, j, rel_bias, diff_lambda[j], diff_norm_g[j],
                                lam_init)
                o = jnp.pad(o, ((0, 0), (0, t - valid), (0, 0))).reshape(m, d)
            z = _res_mm(o, diff_w_o, (j,), xf, 1.0)
        xf, xb = _layer_norm(z, ln_g[i, 1], ln_b[i, 1])
        xf, xb = ffn(xf, xb, i, 1)
    return xf, jnp.stack(gla_states), jnp.stack(k_rows), jnp.stack(v_rows)


def kernel(x_prompt, x_sample, state_gla, cache_k, cache_v, page_table, ln_g, ln_b, ffn_w_gate, ffn_w_up,
           ffn_w_down, gla_w_in, gla_w_a2, gla_b_a, gla_norm_g, gla_w_o, diff_w_in, diff_lambda, diff_norm_g,
           diff_w_o, rel_bias):
    prm = (ln_g, ln_b, ffn_w_gate, ffn_w_up, ffn_w_down, gla_w_in, gla_w_a2, gla_b_a, gla_norm_g, gla_w_o,
           diff_w_in, diff_lambda, diff_norm_g, diff_w_o, rel_bias)
    nb, t, d = x_prompt.shape
    hw = d // DIFF_HEADS
    y, gs, kr, vr = _run(x_prompt.reshape(nb * t, d), nb, t, None, None, prm)
    y_prompt = y.reshape(nb, t, d)
    k_prompt = kr.reshape(-1, nb, t, DIFF_HEADS, hw)
    v_prompt = vr.reshape(-1, nb, t, DIFF_HEADS, hw)

    sb, st, _ = x_sample.shape
    xs = jnp.pad(x_sample, ((0, 0), (0, SAMPLE_ROWS - st), (0, 0))).reshape(sb * SAMPLE_ROWS, d)
    ys, gss, krs, vrs = _run(xs, sb, st, state_gla, (cache_k, cache_v, page_table), prm)
    unpad = lambda a: a.reshape(a.shape[:-2] + (sb, SAMPLE_ROWS, d))[..., :st, :]
    y_sample = unpad(ys)
    k_sample = unpad(krs).reshape(-1, sb, st, DIFF_HEADS, hw)
    v_sample = unpad(vrs).reshape(-1, sb, st, DIFF_HEADS, hw)
    return (y_prompt, y_sample, gs, gss, k_prompt, v_prompt, k_sample, v_sample)
```

```python
import functools
import math

import jax
import jax.numpy as jnp
import numpy as np
from jax import lax
from jax.experimental import pallas as pl
from jax.experimental.pallas import tpu as pltpu

F32 = jnp.float32
BF16 = jnp.bfloat16

DEPTH = 2
N_MIXERS = 2
GLA_HEADS = 4
GLA_GATE_RANK = 16
GLA_TAU = 16.0
GLA_CHUNK = 64
DIFF_HEADS = 16
PAGE_SIZE = 128
REL_BUCKETS = 32
REL_MAX_DIST = 128
DEEPNORM_ALPHA = (2.0 * DEPTH) ** 0.25
LN_EPS = 1e-5
NEG = -1e30

LANES = 128
MIB = 1024 * 1024
SPLIT_K_ABOVE = 8192
FLASH_TILE = 512
SAMPLE_ROWS = 16
SAMPLE_KEY_ROWS = 8


def _row_tile(m, cap):
    for t in range(min(m, cap) // 16 * 16, 15, -16):
        if m % t == 0:
            return t
    raise ValueError(f"no row tile for {m} rows")


def _params(sem, vmem_mib):
    return pltpu.CompilerParams(dimension_semantics=sem, vmem_limit_bytes=vmem_mib * MIB)


def _ln_kernel(z_ref, g_ref, b_ref, xf_ref, xb_ref):
    z = z_ref[...]
    zc = z - jnp.mean(z, axis=-1, keepdims=True)
    var = jnp.mean(zc * zc, axis=-1, keepdims=True)
    y = zc * lax.rsqrt(var + LN_EPS) * g_ref[...] + b_ref[...]
    xf_ref[...] = y
    xb_ref[...] = y.astype(BF16)


def _layer_norm(z, g, b):
    m, d = z.shape
    tr = _row_tile(m, 256)
    return pl.pallas_call(
        _ln_kernel,
        grid=(m // tr,),
        in_specs=[pl.BlockSpec((tr, d), lambda i: (i, 0)),
                  pl.BlockSpec((1, d), lambda i: (0, 0)),
                  pl.BlockSpec((1, d), lambda i: (0, 0))],
        out_specs=[pl.BlockSpec((tr, d), lambda i: (i, 0)),
                   pl.BlockSpec((tr, d), lambda i: (i, 0))],
        out_shape=[jax.ShapeDtypeStruct((m, d), F32), jax.ShapeDtypeStruct((m, d), BF16)],
        compiler_params=_params(("arbitrary",), 40),
        name="layer_norm",
    )(z, g.reshape(1, d), b.reshape(1, d))


def _cast_kernel(x_ref, o_ref):
    o_ref[...] = x_ref[...].astype(BF16)


def _to_bf16(x):
    m, d = x.shape
    tr = _row_tile(m, 256)
    return pl.pallas_call(
        _cast_kernel,
        grid=(m // tr,),
        in_specs=[pl.BlockSpec((tr, d), lambda i: (i, 0))],
        out_specs=pl.BlockSpec((tr, d), lambda i: (i, 0)),
        out_shape=jax.ShapeDtypeStruct((m, d), BF16),
        compiler_params=_params(("arbitrary",), 40),
        name="to_bf16",
    )(x)


def _w_spec(w, lead, k, tn, col0):
    none = (None,) * len(lead)
    return pl.BlockSpec(none + (k, tn), lambda n, m: tuple(lead) + (0, col0 + n))


def _row_specs(tm, ms, width, col):
    return [pl.BlockSpec((tm, width), lambda n, i: (i, col(n))),
            pl.BlockSpec((ms, width), lambda n, i: (0, col(n)))]


def _pair_shapes(m, ms, n_cols, dtype):
    return [jax.ShapeDtypeStruct((m, n_cols), dtype), jax.ShapeDtypeStruct((ms, n_cols), dtype)]


def _proj_kernel(x_ref, xs_ref, w_ref, o_ref, os_ref, wb_ref):
    @pl.when(pl.program_id(1) == 0)
    def _():
        wb_ref[...] = w_ref[...].astype(BF16)
        os_ref[...] = jnp.dot(xs_ref[...], wb_ref[...], preferred_element_type=F32).astype(os_ref.dtype)

    o_ref[...] = jnp.dot(x_ref[...], wb_ref[...], preferred_element_type=F32).astype(o_ref.dtype)


def _proj(x, w, lead, col0, n_cols, tn, out_dtype=F32):
    (m, k), ms = x[0].shape, x[1].shape[0]
    tm = _row_tile(m, 1024)
    return pl.pallas_call(
        _proj_kernel,
        grid=(n_cols // tn, m // tm),
        in_specs=_row_specs(tm, ms, k, lambda n: 0) + [_w_spec(w, lead, k, tn, col0)],
        out_specs=_row_specs(tm, ms, tn, lambda n: n),
        out_shape=_pair_shapes(m, ms, n_cols, out_dtype),
        scratch_shapes=[pltpu.VMEM((k, tn), BF16)],
        compiler_params=_params(("arbitrary", "arbitrary"), 56),
        name="proj",
    )(x[0], x[1], w)


def _ffn_up_kernel(x_ref, xs_ref, wg_ref, wu_ref, h_ref, hs_ref, wgb_ref, wub_ref):
    def swiglu(x):
        g = jnp.dot(x, wgb_ref[...], preferred_element_type=F32)
        u = jnp.dot(x, wub_ref[...], preferred_element_type=F32)
        return (g * jax.nn.sigmoid(g) * u).astype(BF16)

    @pl.when(pl.program_id(1) == 0)
    def _():
        wgb_ref[...] = wg_ref[...].astype(BF16)
        wub_ref[...] = wu_ref[...].astype(BF16)
        hs_ref[...] = swiglu(xs_ref[...])

    h_ref[...] = swiglu(x_ref[...])


def _ffn_up(x, w_gate, w_up, lead):
    (m, k), ms = x[0].shape, x[1].shape[0]
    f = w_gate.shape[-1]
    tm = _row_tile(m, 1024)
    tn = 256
    return pl.pallas_call(
        _ffn_up_kernel,
        grid=(f // tn, m // tm),
        in_specs=_row_specs(tm, ms, k, lambda n: 0)
                 + [_w_spec(w_gate, lead, k, tn, 0), _w_spec(w_up, lead, k, tn, 0)],
        out_specs=_row_specs(tm, ms, tn, lambda n: n),
        out_shape=_pair_shapes(m, ms, f, BF16),
        scratch_shapes=[pltpu.VMEM((k, tn), BF16), pltpu.VMEM((k, tn), BF16)],
        compiler_params=_params(("arbitrary", "arbitrary"), 56),
        name="ffn_up",
    )(x[0], x[1], w_gate, w_up)


def _res_mm_kernel(*refs, alpha, scale, has_partial, final):
    h_ref, hs_ref, w_ref = refs[:3]
    o_ref, os_ref, wb_ref = refs[-3:]
    extra = refs[3:-3]

    def epilogue(y, group):
        ops = list(extra[group::2])
        if has_partial:
            y = ops.pop(0)[...] + y
        if final:
            y = alpha * ops.pop(0)[...] + scale * y
        return y

    @pl.when(pl.program_id(1) == 0)
    def _():
        wb_ref[...] = w_ref[...].astype(BF16)
        os_ref[...] = epilogue(jnp.dot(hs_ref[...], wb_ref[...], preferred_element_type=F32), 1)

    o_ref[...] = epilogue(jnp.dot(h_ref[...], wb_ref[...], preferred_element_type=F32), 0)


def _res_mm_pass(h, w, lead, kb, n_kb, partial, res, scale, tm_cap):
    (m, k), ms = h[0].shape, h[1].shape[0]
    kh = k // n_kb
    n_cols = w.shape[-1]
    tn = 512
    tm = _row_tile(m, tm_cap)
    none = (None,) * len(lead)
    in_specs = _row_specs(tm, ms, kh, lambda n: kb) + [
        pl.BlockSpec(none + (kh, tn), lambda n, i: tuple(lead) + (kb, n))]
    args = [h[0], h[1], w]
    for extra in (partial, res):
        if extra is not None:
            in_specs += _row_specs(tm, ms, tn, lambda n: n)
            args += list(extra)
    return pl.pallas_call(
        functools.partial(_res_mm_kernel, alpha=DEEPNORM_ALPHA, scale=scale,
                          has_partial=partial is not None, final=res is not None),
        grid=(n_cols // tn, m // tm),
        in_specs=in_specs,
        out_specs=_row_specs(tm, ms, tn, lambda n: n),
        out_shape=_pair_shapes(m, ms, n_cols, F32),
        scratch_shapes=[pltpu.VMEM((kh, tn), BF16)],
        compiler_params=_params(("arbitrary", "arbitrary"), 56),
        name="res_mm",
    )(*args)


def _res_mm(h, w, lead, res, scale):
    k = h[0].shape[1]
    if k > SPLIT_K_ABOVE and (k // 2) % LANES == 0:
        part = _res_mm_pass(h, w, lead, 0, 2, None, None, scale, 512)
        return _res_mm_pass(h, w, lead, 1, 2, part, res, scale, 512)
    return _res_mm_pass(h, w, lead, 0, 1, None, res, scale, 1024)


def _log_sigmoid(x):
    return jnp.minimum(x, 0.0) - jnp.log(1.0 + jnp.exp(-jnp.abs(x)))


def _gla_kernel(*refs, chunk, valid, has_s0, q_scale, dk, dv):
    if has_s0:
        (q_ref, k_ref, v_ref, r_ref, glr_ref, wa2_ref, ba_ref, ng_ref, s0_ref,
         o_ref, sfin_ref, s_ref) = refs
    else:
        (q_ref, k_ref, v_ref, r_ref, glr_ref, wa2_ref, ba_ref, ng_ref,
         o_ref, sfin_ref, s_ref) = refs
    c = pl.program_id(1)

    @pl.when(c == 0)
    def _():
        if has_s0:
            s_ref[...] = s0_ref[...]
        else:
            s_ref[...] = jnp.zeros_like(s_ref)

    hk = q_ref.shape[-1]
    pre = jnp.dot(glr_ref[...].astype(BF16), wa2_ref[...].astype(BF16),
                  preferred_element_type=F32) + ba_ref[...]
    log_a = _log_sigmoid(pre) / GLA_TAU
    k = k_ref[...]
    if valid < chunk:
        row_ok = lax.broadcasted_iota(jnp.int32, (chunk, 1), 0) < valid
        log_a = jnp.where(row_ok, log_a, 0.0)
        k = jnp.where(row_ok, k, 0.0)
    ri = lax.broadcasted_iota(jnp.int32, (chunk, chunk), 0)
    ci = lax.broadcasted_iota(jnp.int32, (chunk, chunk), 1)
    causal = ri >= ci
    b = jnp.dot(jnp.where(causal, 1.0, 0.0), log_a, precision=lax.Precision.HIGHEST,
                preferred_element_type=F32)
    b_last = b[chunk - 1:chunk, :]
    q_t = (q_ref[...] * q_scale * jnp.exp(b)).astype(BF16)
    k_t = (k * jnp.exp(-b)).astype(BF16)
    k_end = (k * jnp.exp(b_last - b)).astype(BF16)
    decay_col = jnp.exp(jnp.broadcast_to(b_last, (LANES, hk)).T[:, :1])
    last = c == pl.num_programs(1) - 1
    for h in range(GLA_HEADS):
        ks = slice(h * dk, (h + 1) * dk)
        vs = slice(h * dv, (h + 1) * dv)
        v = v_ref[:, vs].astype(BF16)
        att = lax.dot_general(q_t[:, ks], k_t[:, ks], (((1,), (1,)), ((), ())),
                              preferred_element_type=F32)
        att = jnp.where(causal, att, 0.0).astype(BF16)
        s_old = s_ref[h]
        o = (jnp.dot(att, v, preferred_element_type=F32)
             + jnp.dot(q_t[:, ks], s_old.astype(BF16), preferred_element_type=F32))
        kv = lax.dot_general(k_end[:, ks], v, (((0,), (0,)), ((), ())), preferred_element_type=F32)
        s_new = decay_col[ks] * s_old + kv
        s_ref[h] = s_new

        @pl.when(last)
        def _(h=h, s_new=s_new):
            sfin_ref[h] = s_new

        oc = o - jnp.mean(o, axis=-1, keepdims=True)
        var = jnp.mean(oc * oc, axis=-1, keepdims=True)
        on = oc * lax.rsqrt(var + LN_EPS) * ng_ref[...]
        r = r_ref[:, vs]
        o_ref[:, vs] = (on * (r * jax.nn.sigmoid(r))).astype(BF16)


def _gla_core(qkvr, glr, w_a2p, b_a, norm_g, s0, n_seq, valid):
    m = qkvr.shape[0]
    hk = w_a2p.shape[-1]
    dk = hk // GLA_HEADS
    dv = norm_g.shape[-1]
    t = m // n_seq
    chunk = GLA_CHUNK if t % GLA_CHUNK == 0 else t
    nc = t // chunk
    hv = GLA_HEADS * dv
    assert (2 * hk) % hv == 0
    vb0 = 2 * hk // hv
    has_s0 = s0 is not None
    row = lambda b, c: b * nc + c
    state = pl.BlockSpec((None, GLA_HEADS, dk, dv), lambda b, c: (b, 0, 0, 0))
    in_specs = [
        pl.BlockSpec((chunk, hk), lambda b, c: (row(b, c), 0)),
        pl.BlockSpec((chunk, hk), lambda b, c: (row(b, c), 1)),
        pl.BlockSpec((chunk, hv), lambda b, c: (row(b, c), vb0)),
        pl.BlockSpec((chunk, hv), lambda b, c: (row(b, c), vb0 + 1)),
        pl.BlockSpec((chunk, LANES), lambda b, c: (row(b, c), 0)),
        pl.BlockSpec((LANES, hk), lambda b, c: (0, 0)),
        pl.BlockSpec((1, hk), lambda b, c: (0, 0)),
        pl.BlockSpec((1, dv), lambda b, c: (0, 0)),
    ]
    args = [qkvr, qkvr, qkvr, qkvr, glr, w_a2p, b_a.reshape(1, hk), norm_g.reshape(1, dv)]
    if has_s0:
        in_specs.append(state)
        args.append(s0)
    return pl.pallas_call(
        functools.partial(_gla_kernel, chunk=chunk, valid=valid, has_s0=has_s0, q_scale=dk ** -0.5,
                          dk=dk, dv=dv),
        grid=(n_seq, nc),
        in_specs=in_specs,
        out_specs=[pl.BlockSpec((chunk, hv), lambda b, c: (row(b, c), 0)), state],
        out_shape=[jax.ShapeDtypeStruct((m, hv), BF16),
                   jax.ShapeDtypeStruct((n_seq, GLA_HEADS, dk, dv), F32)],
        scratch_shapes=[pltpu.VMEM((GLA_HEADS, dk, dv), F32)],
        compiler_params=_params(("arbitrary", "arbitrary"), 56),
        name="gla_core",
    )(*args)


def _t5_bucket(dist):
    n = jnp.maximum(dist, 0)
    max_exact = REL_BUCKETS // 2
    large = max_exact + (jnp.log(jnp.maximum(n, 1).astype(F32) / max_exact)
                         / math.log(REL_MAX_DIST / max_exact) * (REL_BUCKETS - max_exact)).astype(jnp.int32)
    large = jnp.minimum(large, REL_BUCKETS - 1)
    return jnp.where(n < max_exact, n, large)


def _bias_kernel(qpos_ref, kpos_ref, qgrp_ref, kgrp_ref, rb_ref, o_ref):
    bucket = _t5_bucket(qpos_ref[...] - kpos_ref[...])
    rb = rb_ref[...]
    acc = jnp.zeros(o_ref.shape, F32)
    for i in range(REL_BUCKETS):
        acc = jnp.where(bucket == i, rb[:, i:i + 1], acc)
    o_ref[...] = jnp.where(qgrp_ref[...] == kgrp_ref[...], acc, NEG)


def _bias_table(qpos, kpos, rb_rows, tr, tc, qgrp=None, kgrp=None):
    r, c = qpos.shape[0], kpos.shape[0]
    qgrp = jnp.zeros((r,), jnp.int32) if qgrp is None else qgrp
    kgrp = jnp.zeros((c,), jnp.int32) if kgrp is None else kgrp
    col = lambda a: a.reshape(r, 1).astype(jnp.int32)
    row = lambda a: a.reshape(1, c).astype(jnp.int32)
    return pl.pallas_call(
        _bias_kernel,
        grid=(r // tr, c // tc),
        in_specs=[pl.BlockSpec((tr, 1), lambda i, j: (i, 0)),
                  pl.BlockSpec((1, tc), lambda i, j: (0, j)),
                  pl.BlockSpec((tr, 1), lambda i, j: (i, 0)),
                  pl.BlockSpec((1, tc), lambda i, j: (0, j)),
                  pl.BlockSpec((tr, REL_BUCKETS), lambda i, j: (i, 0))],
        out_specs=pl.BlockSpec((tr, tc), lambda i, j: (i, j)),
        out_shape=jax.ShapeDtypeStruct((r, c), F32),
        compiler_params=_params(("arbitrary", "arbitrary"), 32),
        name="t5_bias",
    )(col(qpos), row(kpos), col(qgrp), row(kgrp), rb_rows)


def _first_const_dist(limit):
    n = np.arange(1, limit + 1)
    large = 16 + (np.log(n.astype(np.float32) / np.float32(16)) / np.float32(math.log(8.0))
                  * np.float32(16)).astype(np.int32)
    bucket = np.where(n < 16, n, np.minimum(large, REL_BUCKETS - 1))
    not_last = np.nonzero(bucket != REL_BUCKETS - 1)[0]
    return int(n[not_last[-1]] + 1) if not_last.size else 1


def _lambda(lam_ref, lam_init):
    lp = lam_ref[...]
    a = jnp.sum(lp[0:1] * lp[1:2], axis=-1, keepdims=True)
    b = jnp.sum(lp[2:3] * lp[3:4], axis=-1, keepdims=True)
    return jnp.exp(a) - jnp.exp(b) + lam_init


def _flash_kernel(q_ref, k_ref, v_ref, bias_ref, lam_ref, ng_ref, o_ref,
                  kb_ref, vb_ref, m_ref, l_ref, acc_ref, *, t, hd, n_near, lam_init, scale):
    qi = pl.program_id(2)

    @pl.when(qi == 0)
    def _():
        kb_ref[...] = k_ref[...].astype(BF16)
        vb_ref[...] = v_ref[...].astype(BF16)

    q = q_ref[...]
    lane = lax.broadcasted_iota(jnp.int32, q.shape, 1)
    q2 = jnp.concatenate([jnp.where(lane < hd, q, 0.0), jnp.where(lane >= hd, q, 0.0)],
                         axis=0).astype(BF16)
    m_ref[...] = jnp.full(m_ref.shape, NEG, F32)
    l_ref[...] = jnp.zeros(l_ref.shape, F32)
    acc_ref[...] = jnp.zeros(acc_ref.shape, F32)
    far_bias = bias_ref[n_near - 1, t - 1:t, 0:1]

    def tile(kj, bias, mask):
        rows = pl.ds(pl.multiple_of(kj * t, t), t)
        s = lax.dot_general(q2, kb_ref[rows, :], (((1,), (1,)), ((), ())),
                            preferred_element_type=F32)
        s = s * scale + bias
        if mask is not None:
            s = jnp.where(mask, s, NEG)
        m_old = m_ref[...]
        m_new = jnp.maximum(m_old, jnp.max(s, axis=-1, keepdims=True))
        alpha = jnp.exp(m_old - m_new)
        p = jnp.exp(s - pltpu.repeat(m_new, t // LANES, axis=1))
        l_ref[...] = alpha * l_ref[...] + jnp.sum(p, axis=-1, keepdims=True)
        acc_ref[...] = (pltpu.repeat(alpha, 2 * hd // LANES, axis=1) * acc_ref[...]
                        + jnp.dot(p.astype(BF16), vb_ref[rows, :], preferred_element_type=F32))
        m_ref[...] = m_new

    def far_body(kj, carry):
        tile(kj, far_bias, None)
        return carry

    lax.fori_loop(0, jnp.maximum(qi - (n_near - 1), 0), far_body, 0)
    for off in range(n_near - 1, 0, -1):
        @pl.when(qi >= off)
        def _(off=off):
            b = bias_ref[off]
            tile(qi - off, jnp.concatenate([b, b], axis=0), None)
    ri = lax.broadcasted_iota(jnp.int32, (2 * t, t), 0)
    ci = lax.broadcasted_iota(jnp.int32, (2 * t, t), 1)
    b = bias_ref[0]
    tile(qi, jnp.concatenate([b, b], axis=0), jnp.where(ri >= t, ri - t, ri) >= ci)

    lam = _lambda(lam_ref, lam_init)
    a = acc_ref[...] / pltpu.repeat(l_ref[...], 2 * hd // LANES, axis=1)
    o = a[:t] - lam * a[t:]
    y = o * lax.rsqrt(jnp.mean(o * o, axis=-1, keepdims=True) + LN_EPS) * ng_ref[...]
    o_ref[...] = (y * (1.0 - lam_init)).astype(BF16)


def _flash_diff(q, k, v, rel_bias, lam_params, norm_g, n_seq, lam_init):
    m, d = q.shape
    t_seq = m // n_seq
    hw = d // DIFF_HEADS
    hd = hw // 2
    t = next(c for c in range(FLASH_TILE, 0, -LANES) if t_seq % c == 0)
    nq = t_seq // t
    n_near = min(nq, -(-(_first_const_dist(t_seq) - 1) // t) + 1)
    qpos = (jnp.arange(n_near)[:, None] * t + jnp.arange(t)[None, :]).reshape(-1)
    qpos = jnp.tile(qpos, DIFF_HEADS)
    rb_rows = jnp.repeat(rel_bias.T, n_near * t, axis=0)
    bias = _bias_table(qpos, jnp.arange(t), rb_rows, t, t).reshape(DIFF_HEADS, n_near, t, t)
    return pl.pallas_call(
        functools.partial(_flash_kernel, t=t, hd=hd, n_near=n_near, lam_init=lam_init, scale=hd ** -0.5),
        grid=(n_seq, DIFF_HEADS, nq),
        in_specs=[pl.BlockSpec((t, hw), lambda b, h, i: (b * nq + i, h)),
                  pl.BlockSpec((t_seq, hw), lambda b, h, i: (b, h)),
                  pl.BlockSpec((t_seq, hw), lambda b, h, i: (b, h)),
                  pl.BlockSpec((None, n_near, t, t), lambda b, h, i: (h, 0, 0, 0)),
                  pl.BlockSpec((4, hd), lambda b, h, i: (0, 0)),
                  pl.BlockSpec((1, hw), lambda b, h, i: (0, 0))],
        out_specs=pl.BlockSpec((t, hw), lambda b, h, i: (b * nq + i, h)),
        out_shape=jax.ShapeDtypeStruct((m, d), BF16),
        scratch_shapes=[pltpu.VMEM((t_seq, hw), BF16), pltpu.VMEM((t_seq, hw), BF16),
                        pltpu.VMEM((2 * t, LANES), F32), pltpu.VMEM((2 * t, LANES), F32),
                        pltpu.VMEM((2 * t, hw), F32)],
        compiler_params=_params(("arbitrary", "arbitrary", "arbitrary"), 48),
        name="flash_diff",
    )(q, k, v, bias, lam_params, norm_g.reshape(1, hw))


def _paged_kernel(pt_ref, q_ref, kn_ref, vn_ref, tab_ref, tabn_ref, lam_ref, ng_ref, *rest,
                  pages, q_tok, lam_init, scale):
    k_refs = rest[:pages]
    v_refs = rest[pages:2 * pages]
    o_ref, m_ref, l_ref, acc_ref = rest[2 * pages:]
    s_id = pl.program_id(1)
    rows, hw = q_ref.shape
    half = rows // 2
    q = q_ref[...]

    def scores(kmat):
        return lax.dot_general(q, kmat, (((1,), (1,)), ((), ())), preferred_element_type=F32) * scale

    def update(s, vmat):
        m_old = m_ref[...]
        m_new = jnp.maximum(m_old, jnp.max(s, axis=-1, keepdims=True))
        alpha = jnp.exp(m_old - m_new)
        p = jnp.exp(s - pltpu.repeat(m_new, s.shape[1] // LANES, axis=1))
        l_ref[...] = alpha * l_ref[...] + jnp.sum(p, axis=-1, keepdims=True)
        acc_ref[...] = (pltpu.repeat(alpha, hw // LANES, axis=1) * acc_ref[...]
                        + jnp.dot(p.astype(BF16), vmat, preferred_element_type=F32))
        m_ref[...] = m_new

    @pl.when(s_id == 0)
    def _():
        m_ref[...] = jnp.full(m_ref.shape, NEG, F32)
        l_ref[...] = jnp.zeros(l_ref.shape, F32)
        acc_ref[...] = jnp.zeros(acc_ref.shape, F32)
        s = scores(kn_ref[...].astype(BF16)) + tabn_ref[...]
        tq = lax.broadcasted_iota(jnp.int32, s.shape, 0) % q_tok
        tk = lax.broadcasted_iota(jnp.int32, s.shape, 1) // DIFF_HEADS
        update(jnp.where(tk <= tq, s, NEG), vn_ref[...].astype(BF16))

    kp = jnp.concatenate([r[...].astype(BF16) for r in k_refs], axis=0)
    vp = jnp.concatenate([r[...].astype(BF16) for r in v_refs], axis=0)
    update(scores(kp) + tab_ref[...], vp)

    @pl.when(s_id == pl.num_programs(1) - 1)
    def _():
        a = acc_ref[...] / pltpu.repeat(l_ref[...], hw // LANES, axis=1)
        lam = _lambda(lam_ref, lam_init)
        o = a[:half] - lam * a[half:]
        y = o * lax.rsqrt(jnp.mean(o * o, axis=-1, keepdims=True) + LN_EPS) * ng_ref[...]
        o_ref[...] = (y * (1.0 - lam_init)).astype(o_ref.dtype)


def _paged_diff(q, k_new, v_new, cache_k, cache_v, page_table, layer, rel_bias, lam_params, norm_g,
                lam_init):
    nb, tq, d = q.shape
    tk = k_new.shape[1]
    assert tq <= tk
    hw = d // DIFF_HEADS
    hd = hw // 2
    n_pages = page_table.shape[1]
    past = n_pages * PAGE_SIZE
    pages = 2 if n_pages % 2 == 0 else 1
    n_steps = n_pages // pages
    rows = 2 * DIFF_HEADS * tq
    page_rows = PAGE_SIZE * DIFF_HEADS
    assert pages * PAGE_SIZE + 1 >= _first_const_dist(past + tq)
    q5 = q.reshape(nb, tq, DIFF_HEADS, 2, hd).transpose(0, 3, 2, 1, 4)
    same_c = jnp.arange(2)[:, None] == jnp.arange(2)[None, :]
    qd = jnp.where(same_c[None, :, None, None, :, None], q5[:, :, :, :, None, :], 0.0)
    qd = qd.reshape(nb, rows, hw).astype(BF16)
    k_rows = k_new.reshape(nb, tk * DIFF_HEADS, hw)
    v_rows = v_new.reshape(nb, tk * DIFF_HEADS, hw)
    qpos = jnp.tile(past + jnp.arange(tq), 2 * DIFF_HEADS)
    qgrp = jnp.tile(jnp.repeat(jnp.arange(DIFF_HEADS), tq), 2)
    rb_rows = jnp.tile(jnp.repeat(rel_bias.T, tq, axis=0), (2, 1))
    col_tok = jnp.repeat(jnp.arange(pages * PAGE_SIZE), DIFF_HEADS)
    col_grp = jnp.tile(jnp.arange(DIFF_HEADS), pages * PAGE_SIZE)
    kpos = jnp.concatenate([jnp.zeros_like(col_tok), (n_steps - 1) * pages * PAGE_SIZE + col_tok])
    tab = _bias_table(qpos, kpos, rb_rows, rows, page_rows, qgrp, jnp.tile(col_grp, 2))
    tab = tab.reshape(rows, 2, pages * page_rows).transpose(1, 0, 2)
    tab_new = _bias_table(qpos, past + jnp.repeat(jnp.arange(tk), DIFF_HEADS), rb_rows, rows,
                          tk * DIFF_HEADS, qgrp, jnp.tile(jnp.arange(DIFF_HEADS), tk))
    ck = cache_k.reshape(cache_k.shape[0], cache_k.shape[1], page_rows, hw)
    cv = cache_v.reshape(cache_v.shape[0], cache_v.shape[1], page_rows, hw)

    def page_spec(i):
        return pl.BlockSpec((None, None, page_rows, hw),
                            lambda b, s, pt: (layer, pt[b, s * pages + i], 0, 0))

    grid_spec = pltpu.PrefetchScalarGridSpec(
        num_scalar_prefetch=1,
        grid=(nb, n_steps),
        in_specs=[pl.BlockSpec((None, rows, hw), lambda b, s, pt: (b, 0, 0)),
                  pl.BlockSpec((None, tk * DIFF_HEADS, hw), lambda b, s, pt: (b, 0, 0)),
                  pl.BlockSpec((None, tk * DIFF_HEADS, hw), lambda b, s, pt: (b, 0, 0)),
                  pl.BlockSpec((None, rows, pages * page_rows),
                               lambda b, s, pt: (jnp.where(s == n_steps - 1, 1, 0), 0, 0)),
                  pl.BlockSpec((rows, tk * DIFF_HEADS), lambda b, s, pt: (0, 0)),
                  pl.BlockSpec((4, hd), lambda b, s, pt: (0, 0)),
                  pl.BlockSpec((1, hw), lambda b, s, pt: (0, 0))]
                 + [page_spec(i) for i in range(pages)] + [page_spec(i) for i in range(pages)],
        out_specs=pl.BlockSpec((None, rows // 2, hw), lambda b, s, pt: (b, 0, 0)),
        scratch_shapes=[pltpu.VMEM((rows, LANES), F32), pltpu.VMEM((rows, LANES), F32),
                        pltpu.VMEM((rows, hw), F32)],
    )
    out = pl.pallas_call(
        functools.partial(_paged_kernel, pages=pages, q_tok=tq, lam_init=lam_init, scale=hd ** -0.5),
        grid_spec=grid_spec,
        out_shape=jax.ShapeDtypeStruct((nb, rows // 2, hw), BF16),
        compiler_params=_params(("arbitrary", "arbitrary"), 56),
        name="paged_diff",
    )(page_table, qd, k_rows, v_rows, tab, tab_new, lam_params, norm_g.reshape(1, hw),
      *([ck] * pages), *([cv] * pages))
    return out.reshape(nb, DIFF_HEADS, tq, hw).transpose(0, 2, 1, 3).reshape(nb, tq, d)


def kernel(x_prompt, x_sample, state_gla, cache_k, cache_v, page_table, ln_g, ln_b, ffn_w_gate, ffn_w_up,
           ffn_w_down, gla_w_in, gla_w_a2, gla_b_a, gla_norm_g, gla_w_o, diff_w_in, diff_lambda, diff_norm_g,
           diff_w_o, rel_bias):
    nb, t, d = x_prompt.shape
    sb, st, _ = x_sample.shape
    hw = d // DIFF_HEADS
    assert st <= SAMPLE_KEY_ROWS <= SAMPLE_ROWS
    xs = jnp.pad(x_sample, ((0, 0), (0, SAMPLE_ROWS - st), (0, 0))).reshape(sb * SAMPLE_ROWS, d)
    xf = (x_prompt.reshape(nb * t, d), xs)
    xb = tuple(_to_bf16(a) for a in xf)
    gla_p, gla_s, k_rows, v_rows = [], [], [], []

    def layer_norm(z, g, b):
        outs = [_layer_norm(a, g, b) for a in z]
        return tuple(o[0] for o in outs), tuple(o[1] for o in outs)

    def ffn(xf, xb, i, s):
        h = _ffn_up(xb, ffn_w_gate, ffn_w_up, (i, s))
        z = _res_mm(h, ffn_w_down, (i, s), xf, 0.5)
        return layer_norm(z, ln_g[i, 2 * s], ln_b[i, 2 * s])

    def sample_seq(a, rows):
        return a.reshape(sb, SAMPLE_ROWS, a.shape[-1])[:, :rows]

    def sample_rows(a):
        return jnp.pad(a, ((0, 0), (0, SAMPLE_ROWS - a.shape[1]), (0, 0))).reshape(sb * SAMPLE_ROWS, a.shape[-1])

    for i in range(DEPTH):
        xf, xb = ffn(xf, xb, i, 0)
        j = i // N_MIXERS
        if i % N_MIXERS == 0:
            hk = gla_w_a2.shape[-1]
            hv = gla_w_o.shape[1]
            qkvr = _proj(xb, gla_w_in, (j,), 0, 2 * hk + 2 * hv, 512)
            w_g = jnp.pad(gla_w_in[j][:, 2 * hk + 2 * hv:], ((0, 0), (0, LANES - GLA_GATE_RANK)))
            glr = _proj(xb, w_g[None], (0,), 0, LANES, LANES)
            w_a2p = jnp.pad(gla_w_a2[j], ((0, LANES - GLA_GATE_RANK), (0, 0)))
            o_p, s_p = _gla_core(qkvr[0], glr[0], w_a2p, gla_b_a[j], gla_norm_g[j], None, nb, t)
            chunk = lambda a: jnp.pad(sample_seq(a, SAMPLE_ROWS), ((0, 0), (0, GLA_CHUNK - SAMPLE_ROWS), (0, 0))
                                      ).reshape(sb * GLA_CHUNK, a.shape[-1])
            o_s, s_s = _gla_core(chunk(qkvr[1]), chunk(glr[1]), w_a2p, gla_b_a[j], gla_norm_g[j],
                                 state_gla[j], sb, st)
            o_s = o_s.reshape(sb, GLA_CHUNK, hv)[:, :SAMPLE_ROWS].reshape(sb * SAMPLE_ROWS, hv)
            gla_p.append(s_p)
            gla_s.append(s_s)
            z = _res_mm((o_p, o_s), gla_w_o, (j,), xf, 1.0)
        else:
            lam_init = 0.8 - 0.6 * math.exp(-0.3 * i)
            q = _proj(xb, diff_w_in, (j,), 0, d, 512)
            k = _proj(xb, diff_w_in, (j,), d // 512, d, 512)
            v = _proj(xb, diff_w_in, (j,), 2 * d // 512, d, 512)
            k_rows.append(k)
            v_rows.append(v)
            o_p = _flash_diff(q[0], k[0], v[0], rel_bias, diff_lambda[j], diff_norm_g[j], nb, lam_init)
            o_s = _paged_diff(sample_seq(q[1], st), sample_seq(k[1], SAMPLE_KEY_ROWS),
                              sample_seq(v[1], SAMPLE_KEY_ROWS), cache_k, cache_v, page_table, j, rel_bias,
                              diff_lambda[j], diff_norm_g[j], lam_init)
            z = _res_mm((o_p, sample_rows(o_s)), diff_w_o, (j,), xf, 1.0)
        xf, xb = layer_norm(z, ln_g[i, 1], ln_b[i, 1])
        xf, xb = ffn(xf, xb, i, 1)

    heads = lambda a, n_seq, rows: a.reshape(n_seq, rows, DIFF_HEADS, hw)
    return (xf[0].reshape(nb, t, d),
            sample_seq(xf[1], st),
            jnp.stack(gla_p),
            jnp.stack(gla_s),
            jnp.stack([heads(k[0], nb, t) for k in k_rows]),
            jnp.stack([heads(v[0], nb, t) for v in v_rows]),
            jnp.stack([heads(sample_seq(k[1], st), sb, st) for k in k_rows]),
            jnp.stack([heads(sample_seq(v[1], st), sb, st) for v in v_rows]))
```

```python
import functools
import math

import jax
import jax.numpy as jnp
import numpy as np
from jax import lax
from jax.experimental import pallas as pl
from jax.experimental.pallas import tpu as pltpu

F32 = jnp.float32
BF16 = jnp.bfloat16

DEPTH = 2
N_MIXERS = 2
GLA_HEADS = 4
GLA_GATE_RANK = 16
GLA_TAU = 16.0
GLA_CHUNK = 64
DIFF_HEADS = 16
PAGE_SIZE = 128
REL_BUCKETS = 32
REL_MAX_DIST = 128
DEEPNORM_ALPHA = (2.0 * DEPTH) ** 0.25
LN_EPS = 1e-5
NEG = -1e30
LOG2E = 1.0 / math.log(2.0)

LANES = 128
MIB = 1024 * 1024
SPLIT_K_ABOVE = 8192
FLASH_TILE = 512
PAGES_PER_STEP = 4
SAMPLE_ROWS = 16
SAMPLE_KEY_ROWS = 8


def _row_tile(m, cap):
    for t in range(min(m, cap) // 16 * 16, 15, -16):
        if m % t == 0:
            return t
    raise ValueError(f"no row tile for {m} rows")


def _params(sem, vmem_mib):
    return pltpu.CompilerParams(dimension_semantics=sem, vmem_limit_bytes=vmem_mib * MIB)


def _ln_apply(z, mean, rstd, g, b):
    return (z - mean) * rstd * g + b


def _ln_kernel(z_ref, g_ref, b_ref, *out_refs, final):
    z = z_ref[...]
    mean = jnp.mean(z, axis=-1, keepdims=True)
    zc = z - mean
    rstd = lax.rsqrt(jnp.mean(zc * zc, axis=-1, keepdims=True) + LN_EPS)
    y = _ln_apply(z, mean, rstd, g_ref[...], b_ref[...])
    if final:
        out_refs[0][...] = y
    else:
        xb_ref, mean_ref, rstd_ref = out_refs
        xb_ref[...] = y.astype(BF16)
        mean_ref[...] = jnp.broadcast_to(mean, mean_ref.shape)
        rstd_ref[...] = jnp.broadcast_to(rstd, rstd_ref.shape)


def _layer_norm(z, g, b, final=False):
    m, d = z.shape
    tr = _row_tile(m, 256)
    rows = lambda w: pl.BlockSpec((tr, w), lambda i: (i, 0))
    if final:
        out_specs, out_shape = [rows(d)], [jax.ShapeDtypeStruct((m, d), F32)]
    else:
        out_specs = [rows(d), rows(LANES), rows(LANES)]
        out_shape = [jax.ShapeDtypeStruct((m, d), BF16), jax.ShapeDtypeStruct((m, LANES), F32),
                     jax.ShapeDtypeStruct((m, LANES), F32)]
    return pl.pallas_call(
        functools.partial(_ln_kernel, final=final),
        grid=(m // tr,),
        in_specs=[rows(d), pl.BlockSpec((1, d), lambda i: (0, 0)), pl.BlockSpec((1, d), lambda i: (0, 0))],
        out_specs=out_specs,
        out_shape=out_shape,
        compiler_params=_params(("arbitrary",), 40),
        name="layer_norm",
    )(z, g.reshape(1, d), b.reshape(1, d))


def _cast_kernel(x_ref, o_ref):
    o_ref[...] = x_ref[...].astype(BF16)


def _to_bf16(x):
    m, d = x.shape
    tr = _row_tile(m, 256)
    return pl.pallas_call(
        _cast_kernel,
        grid=(m // tr,),
        in_specs=[pl.BlockSpec((tr, d), lambda i: (i, 0))],
        out_specs=pl.BlockSpec((tr, d), lambda i: (i, 0)),
        out_shape=jax.ShapeDtypeStruct((m, d), BF16),
        compiler_params=_params(("arbitrary",), 40),
        name="to_bf16",
    )(x)


def _w_spec(w, lead, k, tn, col0):
    none = (None,) * len(lead)
    return pl.BlockSpec(none + (k, tn), lambda n, m: tuple(lead) + (0, col0 + n))


def _row_specs(tm, ms, width, col):
    return [pl.BlockSpec((tm, width), lambda n, i: (i, col(n))),
            pl.BlockSpec((ms, width), lambda n, i: (0, col(n)))]


def _pair_shapes(m, ms, n_cols, dtype):
    return [jax.ShapeDtypeStruct((m, n_cols), dtype), jax.ShapeDtypeStruct((ms, n_cols), dtype)]


def _proj_kernel(x_ref, xs_ref, w_ref, o_ref, os_ref, wb_ref):
    @pl.when(pl.program_id(1) == 0)
    def _():
        wb_ref[...] = w_ref[...].astype(BF16)
        os_ref[...] = jnp.dot(xs_ref[...], wb_ref[...], preferred_element_type=F32).astype(os_ref.dtype)

    o_ref[...] = jnp.dot(x_ref[...], wb_ref[...], preferred_element_type=F32).astype(o_ref.dtype)


def _proj(x, w, lead, col0, n_cols, tn, out_dtype=F32):
    (m, k), ms = x[0].shape, x[1].shape[0]
    tm = _row_tile(m, 1024)
    return pl.pallas_call(
        _proj_kernel,
        grid=(n_cols // tn, m // tm),
        in_specs=_row_specs(tm, ms, k, lambda n: 0) + [_w_spec(w, lead, k, tn, col0)],
        out_specs=_row_specs(tm, ms, tn, lambda n: n),
        out_shape=_pair_shapes(m, ms, n_cols, out_dtype),
        scratch_shapes=[pltpu.VMEM((k, tn), BF16)],
        compiler_params=_params(("arbitrary", "arbitrary"), 56),
        name="proj",
    )(x[0], x[1], w)


def _ffn_up_kernel(x_ref, xs_ref, wg_ref, wu_ref, h_ref, hs_ref, wgb_ref, wub_ref):
    def swiglu(x):
        g = jnp.dot(x, wgb_ref[...], preferred_element_type=F32)
        u = jnp.dot(x, wub_ref[...], preferred_element_type=F32)
        return (g * jax.nn.sigmoid(g) * u).astype(BF16)

    @pl.when(pl.program_id(1) == 0)
    def _():
        wgb_ref[...] = wg_ref[...].astype(BF16)
        wub_ref[...] = wu_ref[...].astype(BF16)
        hs_ref[...] = swiglu(xs_ref[...])

    h_ref[...] = swiglu(x_ref[...])


def _ffn_up(x, w_gate, w_up, lead):
    (m, k), ms = x[0].shape, x[1].shape[0]
    f = w_gate.shape[-1]
    tm = _row_tile(m, 1024)
    tn = 256
    return pl.pallas_call(
        _ffn_up_kernel,
        grid=(f // tn, m // tm),
        in_specs=_row_specs(tm, ms, k, lambda n: 0)
                 + [_w_spec(w_gate, lead, k, tn, 0), _w_spec(w_up, lead, k, tn, 0)],
        out_specs=_row_specs(tm, ms, tn, lambda n: n),
        out_shape=_pair_shapes(m, ms, f, BF16),
        scratch_shapes=[pltpu.VMEM((k, tn), BF16), pltpu.VMEM((k, tn), BF16)],
        compiler_params=_params(("arbitrary", "arbitrary"), 56),
        name="ffn_up",
    )(x[0], x[1], w_gate, w_up)


def _res_mm_kernel(*refs, alpha, scale, has_partial, res_mode):
    h_ref, hs_ref, w_ref = refs[:3]
    o_ref, os_ref, wb_ref = refs[-3:]
    extra = list(refs[3:-3])
    partial = [extra.pop(0), extra.pop(0)] if has_partial else None

    def residual(group):
        if res_mode == "plain":
            return extra[group][...]
        z, mean, rstd = extra[group], extra[2 + group], extra[4 + group]
        wide = lambda a: pltpu.repeat(a[...], z.shape[1] // LANES, axis=1)
        return _ln_apply(z[...], wide(mean), wide(rstd), extra[6][...], extra[7][...])

    def epilogue(y, group):
        if has_partial:
            y = partial[group][...] + y
        if res_mode is not None:
            y = alpha * residual(group) + scale * y
        return y

    @pl.when(pl.program_id(1) == 0)
    def _():
        wb_ref[...] = w_ref[...].astype(BF16)
        os_ref[...] = epilogue(jnp.dot(hs_ref[...], wb_ref[...], preferred_element_type=F32), 1)

    o_ref[...] = epilogue(jnp.dot(h_ref[...], wb_ref[...], preferred_element_type=F32), 0)


def _res_mm_pass(h, w, lead, kb, n_kb, partial, res, scale, tm_cap):
    (m, k), ms = h[0].shape, h[1].shape[0]
    kh = k // n_kb
    n_cols = w.shape[-1]
    tn = 512
    tm = _row_tile(m, tm_cap)
    none = (None,) * len(lead)
    tiles = lambda: _row_specs(tm, ms, tn, lambda n: n)
    in_specs = _row_specs(tm, ms, kh, lambda n: kb) + [
        pl.BlockSpec(none + (kh, tn), lambda n, i: tuple(lead) + (kb, n))]
    args = [h[0], h[1], w]
    if partial is not None:
        in_specs += tiles()
        args += list(partial)
    if res is not None and res[0] == "plain":
        in_specs += tiles()
        args += list(res[1])
    elif res is not None:
        _, z, mean, rstd, g, b = res
        vec = pl.BlockSpec((1, tn), lambda n, i: (0, n))
        in_specs += tiles() + 2 * _row_specs(tm, ms, LANES, lambda n: 0) + [vec, vec]
        args += list(z) + list(mean) + list(rstd) + [g.reshape(1, n_cols), b.reshape(1, n_cols)]
    return pl.pallas_call(
        functools.partial(_res_mm_kernel, alpha=DEEPNORM_ALPHA, scale=scale,
                          has_partial=partial is not None, res_mode=None if res is None else res[0]),
        grid=(n_cols // tn, m // tm),
        in_specs=in_specs,
        out_specs=_row_specs(tm, ms, tn, lambda n: n),
        out_shape=_pair_shapes(m, ms, n_cols, F32),
        scratch_shapes=[pltpu.VMEM((kh, tn), BF16)],
        compiler_params=_params(("arbitrary", "arbitrary"), 56),
        name="res_mm",
    )(*args)


def _res_mm(h, w, lead, res, scale):
    k = h[0].shape[1]
    if k > SPLIT_K_ABOVE and (k // 2) % LANES == 0:
        part = _res_mm_pass(h, w, lead, 0, 2, None, None, scale, 512)
        return _res_mm_pass(h, w, lead, 1, 2, part, res, scale, 512)
    return _res_mm_pass(h, w, lead, 0, 1, None, res, scale, 1024)


def _log_sigmoid(x):
    return jnp.minimum(x, 0.0) - jnp.log(1.0 + jnp.exp(-jnp.abs(x)))


def _gla_kernel(*refs, chunk, valid, has_s0, q_scale, dk, dv):
    if has_s0:
        (q_ref, k_ref, v_ref, r_ref, glr_ref, wa2_ref, ba_ref, ng_ref, s0_ref,
         o_ref, sfin_ref, s_ref) = refs
    else:
        (q_ref, k_ref, v_ref, r_ref, glr_ref, wa2_ref, ba_ref, ng_ref,
         o_ref, sfin_ref, s_ref) = refs
    c = pl.program_id(1)

    @pl.when(c == 0)
    def _():
        if has_s0:
            s_ref[...] = s0_ref[...]
        else:
            s_ref[...] = jnp.zeros_like(s_ref)

    hk = q_ref.shape[-1]
    pre = jnp.dot(glr_ref[...].astype(BF16), wa2_ref[...].astype(BF16),
                  preferred_element_type=F32) + ba_ref[...]
    log_a = _log_sigmoid(pre) / GLA_TAU
    k = k_ref[...]
    if valid < chunk:
        row_ok = lax.broadcasted_iota(jnp.int32, (chunk, 1), 0) < valid
        log_a = jnp.where(row_ok, log_a, 0.0)
        k = jnp.where(row_ok, k, 0.0)
    ri = lax.broadcasted_iota(jnp.int32, (chunk, chunk), 0)
    ci = lax.broadcasted_iota(jnp.int32, (chunk, chunk), 1)
    causal = ri >= ci
    b = jnp.dot(jnp.where(causal, 1.0, 0.0), log_a, precision=lax.Precision.HIGHEST,
                preferred_element_type=F32)
    b_last = b[chunk - 1:chunk, :]
    q_t = (q_ref[...] * q_scale * jnp.exp(b)).astype(BF16)
    k_t = (k * jnp.exp(-b)).astype(BF16)
    k_end = (k * jnp.exp(b_last - b)).astype(BF16)
    decay_col = jnp.exp(jnp.broadcast_to(b_last, (LANES, hk)).T[:, :1])
    last = c == pl.num_programs(1) - 1
    for h in range(GLA_HEADS):
        ks = slice(h * dk, (h + 1) * dk)
        vs = slice(h * dv, (h + 1) * dv)
        v = v_ref[:, vs].astype(BF16)
        att = lax.dot_general(q_t[:, ks], k_t[:, ks], (((1,), (1,)), ((), ())),
                              preferred_element_type=F32)
        att = jnp.where(causal, att, 0.0).astype(BF16)
        s_old = s_ref[h]
        o = (jnp.dot(att, v, preferred_element_type=F32)
             + jnp.dot(q_t[:, ks], s_old.astype(BF16), preferred_element_type=F32))
        kv = lax.dot_general(k_end[:, ks], v, (((0,), (0,)), ((), ())), preferred_element_type=F32)
        s_new = decay_col[ks] * s_old + kv
        s_ref[h] = s_new

        @pl.when(last)
        def _(h=h, s_new=s_new):
            sfin_ref[h] = s_new

        oc = o - jnp.mean(o, axis=-1, keepdims=True)
        var = jnp.mean(oc * oc, axis=-1, keepdims=True)
        on = oc * lax.rsqrt(var + LN_EPS) * ng_ref[...]
        r = r_ref[:, vs]
        o_ref[:, vs] = (on * (r * jax.nn.sigmoid(r))).astype(BF16)


def _gla_core(qkvr, glr, w_a2p, b_a, norm_g, s0, n_seq, valid):
    m = qkvr.shape[0]
    hk = w_a2p.shape[-1]
    dk = hk // GLA_HEADS
    dv = norm_g.shape[-1]
    t = m // n_seq
    chunk = GLA_CHUNK if t % GLA_CHUNK == 0 else t
    nc = t // chunk
    hv = GLA_HEADS * dv
    assert (2 * hk) % hv == 0
    vb0 = 2 * hk // hv
    has_s0 = s0 is not None
    row = lambda b, c: b * nc + c
    state = pl.BlockSpec((None, GLA_HEADS, dk, dv), lambda b, c: (b, 0, 0, 0))
    in_specs = [
        pl.BlockSpec((chunk, hk), lambda b, c: (row(b, c), 0)),
        pl.BlockSpec((chunk, hk), lambda b, c: (row(b, c), 1)),
        pl.BlockSpec((chunk, hv), lambda b, c: (row(b, c), vb0)),
        pl.BlockSpec((chunk, hv), lambda b, c: (row(b, c), vb0 + 1)),
        pl.BlockSpec((chunk, LANES), lambda b, c: (row(b, c), 0)),
        pl.BlockSpec((LANES, hk), lambda b, c: (0, 0)),
        pl.BlockSpec((1, hk), lambda b, c: (0, 0)),
        pl.BlockSpec((1, dv), lambda b, c: (0, 0)),
    ]
    args = [qkvr, qkvr, qkvr, qkvr, glr, w_a2p, b_a.reshape(1, hk), norm_g.reshape(1, dv)]
    if has_s0:
        in_specs.append(state)
        args.append(s0)
    return pl.pallas_call(
        functools.partial(_gla_kernel, chunk=chunk, valid=valid, has_s0=has_s0, q_scale=dk ** -0.5,
                          dk=dk, dv=dv),
        grid=(n_seq, nc),
        in_specs=in_specs,
        out_specs=[pl.BlockSpec((chunk, hv), lambda b, c: (row(b, c), 0)), state],
        out_shape=[jax.ShapeDtypeStruct((m, hv), BF16),
                   jax.ShapeDtypeStruct((n_seq, GLA_HEADS, dk, dv), F32)],
        scratch_shapes=[pltpu.VMEM((GLA_HEADS, dk, dv), F32)],
        compiler_params=_params(("arbitrary", "arbitrary"), 56),
        name="gla_core",
    )(*args)


def _t5_bucket(dist):
    n = jnp.maximum(dist, 0)
    max_exact = REL_BUCKETS // 2
    large = max_exact + (jnp.log(jnp.maximum(n, 1).astype(F32) / max_exact)
                         / math.log(REL_MAX_DIST / max_exact) * (REL_BUCKETS - max_exact)).astype(jnp.int32)
    large = jnp.minimum(large, REL_BUCKETS - 1)
    return jnp.where(n < max_exact, n, large)


def _bias_kernel(qpos_ref, kpos_ref, qgrp_ref, kgrp_ref, rb_ref, o_ref, *, mult):
    bucket = _t5_bucket(qpos_ref[...] - kpos_ref[...])
    rb = rb_ref[...]
    acc = jnp.zeros(o_ref.shape, F32)
    for i in range(REL_BUCKETS):
        acc = jnp.where(bucket == i, rb[:, i:i + 1], acc)
    if mult != 1.0:
        acc = acc * mult
    o_ref[...] = jnp.where(qgrp_ref[...] == kgrp_ref[...], acc, NEG)


def _bias_table(qpos, kpos, rb_rows, tr, tc, qgrp=None, kgrp=None, mult=1.0):
    r, c = qpos.shape[0], kpos.shape[0]
    qgrp = jnp.zeros((r,), jnp.int32) if qgrp is None else qgrp
    kgrp = jnp.zeros((c,), jnp.int32) if kgrp is None else kgrp
    col = lambda a: a.reshape(r, 1).astype(jnp.int32)
    row = lambda a: a.reshape(1, c).astype(jnp.int32)
    return pl.pallas_call(
        functools.partial(_bias_kernel, mult=mult),
        grid=(r // tr, c // tc),
        in_specs=[pl.BlockSpec((tr, 1), lambda i, j: (i, 0)),
                  pl.BlockSpec((1, tc), lambda i, j: (0, j)),
                  pl.BlockSpec((tr, 1), lambda i, j: (i, 0)),
                  pl.BlockSpec((1, tc), lambda i, j: (0, j)),
                  pl.BlockSpec((tr, REL_BUCKETS), lambda i, j: (i, 0))],
        out_specs=pl.BlockSpec((tr, tc), lambda i, j: (i, j)),
        out_shape=jax.ShapeDtypeStruct((r, c), F32),
        compiler_params=_params(("arbitrary", "arbitrary"), 32),
        name="t5_bias",
    )(col(qpos), row(kpos), col(qgrp), row(kgrp), rb_rows)


def _first_const_dist(limit):
    n = np.arange(1, limit + 1)
    large = 16 + (np.log(n.astype(np.float32) / np.float32(16)) / np.float32(math.log(8.0))
                  * np.float32(16)).astype(np.int32)
    bucket = np.where(n < 16, n, np.minimum(large, REL_BUCKETS - 1))
    not_last = np.nonzero(bucket != REL_BUCKETS - 1)[0]
    return int(n[not_last[-1]] + 1) if not_last.size else 1


def _lambda(lam_ref, lam_init):
    lp = lam_ref[...]
    a = jnp.sum(lp[0:1] * lp[1:2], axis=-1, keepdims=True)
    b = jnp.sum(lp[2:3] * lp[3:4], axis=-1, keepdims=True)
    return jnp.exp(a) - jnp.exp(b) + lam_init


def _flash_kernel(q_ref, k_ref, v_ref, bias_ref, lam_ref, ng_ref, o_ref,
                  kb_ref, vb_ref, m_ref, l_ref, acc_ref, *, t, hd, n_near, lam_init, scale):
    qi = pl.program_id(2)

    @pl.when(qi == 0)
    def _():
        kb_ref[...] = k_ref[...].astype(BF16)
        vb_ref[...] = v_ref[...].astype(BF16)

    q = q_ref[...] * (scale * LOG2E)
    lane = lax.broadcasted_iota(jnp.int32, q.shape, 1)
    q2 = jnp.concatenate([jnp.where(lane < hd, q, 0.0), jnp.where(lane >= hd, q, 0.0)],
                         axis=0).astype(BF16)
    m_ref[...] = jnp.full(m_ref.shape, NEG, F32)
    l_ref[...] = jnp.zeros(l_ref.shape, F32)
    acc_ref[...] = jnp.zeros(acc_ref.shape, F32)
    far_bias = bias_ref[n_near - 1, t - 1:t, 0:1]

    def tile(kj, bias, mask):
        rows = pl.ds(pl.multiple_of(kj * t, t), t)
        kt = kb_ref[rows, :]
        vt = vb_ref[rows, :]
        for c in range(2):
            rc = slice(c * t, (c + 1) * t)
            s = lax.dot_general(q2[rc], kt, (((1,), (1,)), ((), ())),
                                preferred_element_type=F32)
            s = s + bias
            if mask is not None:
                s = jnp.where(mask, s, NEG)
            m_old = m_ref[rc, :]
            m_new = jnp.maximum(m_old, jnp.max(s, axis=-1, keepdims=True))
            alpha = jnp.exp2(m_old - m_new)
            p = jnp.exp2(s - pltpu.repeat(m_new, t // LANES, axis=1))
            l_ref[rc, :] = alpha * l_ref[rc, :] + jnp.sum(p, axis=-1, keepdims=True)
            acc_ref[rc, :] = (pltpu.repeat(alpha, 2 * hd // LANES, axis=1) * acc_ref[rc, :]
                              + jnp.dot(p.astype(BF16), vt, preferred_element_type=F32))
            m_ref[rc, :] = m_new

    def far_body(kj, carry):
        tile(kj, far_bias, None)
        return carry

    lax.fori_loop(0, jnp.maximum(qi - (n_near - 1), 0), far_body, 0)
    for off in range(n_near - 1, 0, -1):
        @pl.when(qi >= off)
        def _(off=off):
            tile(qi - off, bias_ref[off], None)
    ri = lax.broadcasted_iota(jnp.int32, (t, t), 0)
    ci = lax.broadcasted_iota(jnp.int32, (t, t), 1)
    tile(qi, bias_ref[0], ri >= ci)

    lam = _lambda(lam_ref, lam_init)
    a = acc_ref[...] / pltpu.repeat(l_ref[...], 2 * hd // LANES, axis=1)
    o = a[:t] - lam * a[t:]
    y = o * lax.rsqrt(jnp.mean(o * o, axis=-1, keepdims=True) + LN_EPS) * ng_ref[...]
    o_ref[...] = (y * (1.0 - lam_init)).astype(BF16)


def _flash_diff(q, k, v, rel_bias, lam_params, norm_g, n_seq, lam_init):
    m, d = q.shape
    t_seq = m // n_seq
    hw = d // DIFF_HEADS
    hd = hw // 2
    t = next(c for c in range(FLASH_TILE, 0, -LANES) if t_seq % c == 0)
    nq = t_seq // t
    n_near = min(nq, -(-(_first_const_dist(t_seq) - 1) // t) + 1)
    qpos = (jnp.arange(n_near)[:, None] * t + jnp.arange(t)[None, :]).reshape(-1)
    qpos = jnp.tile(qpos, DIFF_HEADS)
    rb_rows = jnp.repeat(rel_bias.T, n_near * t, axis=0)
    bias = _bias_table(qpos, jnp.arange(t), rb_rows, t, t, mult=LOG2E).reshape(DIFF_HEADS, n_near, t, t)
    return pl.pallas_call(
        functools.partial(_flash_kernel, t=t, hd=hd, n_near=n_near, lam_init=lam_init, scale=hd ** -0.5),
        grid=(n_seq, DIFF_HEADS, nq),
        in_specs=[pl.BlockSpec((t, hw), lambda b, h, i: (b * nq + i, h)),
                  pl.BlockSpec((t_seq, hw), lambda b, h, i: (b, h)),
                  pl.BlockSpec((t_seq, hw), lambda b, h, i: (b, h)),
                  pl.BlockSpec((None, n_near, t, t), lambda b, h, i: (h, 0, 0, 0)),
                  pl.BlockSpec((4, hd), lambda b, h, i: (0, 0)),
                  pl.BlockSpec((1, hw), lambda b, h, i: (0, 0))],
        out_specs=pl.BlockSpec((t, hw), lambda b, h, i: (b * nq + i, h)),
        out_shape=jax.ShapeDtypeStruct((m, d), BF16),
        scratch_shapes=[pltpu.VMEM((t_seq, hw), BF16), pltpu.VMEM((t_seq, hw), BF16),
                        pltpu.VMEM((2 * t, LANES), F32), pltpu.VMEM((2 * t, LANES), F32),
                        pltpu.VMEM((2 * t, hw), F32)],
        compiler_params=_params(("arbitrary", "arbitrary", "arbitrary"), 48),
        name="flash_diff",
    )(q, k, v, bias, lam_params, norm_g.reshape(1, hw))


def _paged_kernel(pt_ref, q_ref, kn_ref, vn_ref, tab_ref, tabn_ref, lam_ref, ng_ref, *rest,
                  pages, q_tok, lam_init, scale):
    k_refs = rest[:pages]
    v_refs = rest[pages:2 * pages]
    o_ref, m_ref, l_ref, acc_ref = rest[2 * pages:]
    s_id = pl.program_id(1)
    rows, hw = q_ref.shape
    half = rows // 2
    q = q_ref[...]

    def scores(kmat):
        return lax.dot_general(q, kmat, (((1,), (1,)), ((), ())), preferred_element_type=F32) * scale

    def update(s, vmat):
        m_old = m_ref[...]
        m_new = jnp.maximum(m_old, jnp.max(s, axis=-1, keepdims=True))
        alpha = jnp.exp(m_old - m_new)
        p = jnp.exp(s - pltpu.repeat(m_new, s.shape[1] // LANES, axis=1))
        l_ref[...] = alpha * l_ref[...] + jnp.sum(p, axis=-1, keepdims=True)
        acc_ref[...] = (pltpu.repeat(alpha, hw // LANES, axis=1) * acc_ref[...]
                        + jnp.dot(p.astype(BF16), vmat, preferred_element_type=F32))
        m_ref[...] = m_new

    @pl.when(s_id == 0)
    def _():
        m_ref[...] = jnp.full(m_ref.shape, NEG, F32)
        l_ref[...] = jnp.zeros(l_ref.shape, F32)
        acc_ref[...] = jnp.zeros(acc_ref.shape, F32)
        s = scores(kn_ref[...].astype(BF16)) + tabn_ref[...]
        tq = lax.broadcasted_iota(jnp.int32, s.shape, 0) % q_tok
        tk = lax.broadcasted_iota(jnp.int32, s.shape, 1) // DIFF_HEADS
        update(jnp.where(tk <= tq, s, NEG), vn_ref[...].astype(BF16))

    kp = jnp.concatenate([r[...].astype(BF16) for r in k_refs], axis=0)
    vp = jnp.concatenate([r[...].astype(BF16) for r in v_refs], axis=0)
    update(scores(kp) + tab_ref[...], vp)

    @pl.when(s_id == pl.num_programs(1) - 1)
    def _():
        a = acc_ref[...] / pltpu.repeat(l_ref[...], hw // LANES, axis=1)
        lam = _lambda(lam_ref, lam_init)
        o = a[:half] - lam * a[half:]
        y = o * lax.rsqrt(jnp.mean(o * o, axis=-1, keepdims=True) + LN_EPS) * ng_ref[...]
        o_ref[...] = (y * (1.0 - lam_init)).astype(o_ref.dtype)


def _paged_diff(q, k_new, v_new, cache_k, cache_v, page_table, layer, rel_bias, lam_params, norm_g,
                lam_init):
    nb, tq, d = q.shape
    tk = k_new.shape[1]
    assert tq <= tk
    hw = d // DIFF_HEADS
    hd = hw // 2
    n_pages = page_table.shape[1]
    past = n_pages * PAGE_SIZE
    pages = next(p for p in (PAGES_PER_STEP, 2, 1) if n_pages % p == 0)
    n_steps = n_pages // pages
    rows = 2 * DIFF_HEADS * tq
    page_rows = PAGE_SIZE * DIFF_HEADS
    assert pages * PAGE_SIZE + 1 >= _first_const_dist(past + tq)
    q5 = q.reshape(nb, tq, DIFF_HEADS, 2, hd).transpose(0, 3, 2, 1, 4)
    same_c = jnp.arange(2)[:, None] == jnp.arange(2)[None, :]
    qd = jnp.where(same_c[None, :, None, None, :, None], q5[:, :, :, :, None, :], 0.0)
    qd = qd.reshape(nb, rows, hw).astype(BF16)
    k_rows = k_new.reshape(nb, tk * DIFF_HEADS, hw)
    v_rows = v_new.reshape(nb, tk * DIFF_HEADS, hw)
    qpos = jnp.tile(past + jnp.arange(tq), 2 * DIFF_HEADS)
    qgrp = jnp.tile(jnp.repeat(jnp.arange(DIFF_HEADS), tq), 2)
    rb_rows = jnp.tile(jnp.repeat(rel_bias.T, tq, axis=0), (2, 1))
    col_tok = jnp.repeat(jnp.arange(pages * PAGE_SIZE), DIFF_HEADS)
    col_grp = jnp.tile(jnp.arange(DIFF_HEADS), pages * PAGE_SIZE)
    kpos = jnp.concatenate([jnp.zeros_like(col_tok), (n_steps - 1) * pages * PAGE_SIZE + col_tok])
    tab = _bias_table(qpos, kpos, rb_rows, rows, page_rows, qgrp, jnp.tile(col_grp, 2))
    tab = tab.reshape(rows, 2, pages * page_rows).transpose(1, 0, 2)
    tab_new = _bias_table(qpos, past + jnp.repeat(jnp.arange(tk), DIFF_HEADS), rb_rows, rows,
                          tk * DIFF_HEADS, qgrp, jnp.tile(jnp.arange(DIFF_HEADS), tk))
    ck = cache_k.reshape(cache_k.shape[0], cache_k.shape[1], page_rows, hw)
    cv = cache_v.reshape(cache_v.shape[0], cache_v.shape[1], page_rows, hw)

    def page_spec(i):
        return pl.BlockSpec((None, None, page_rows, hw),
                            lambda b, s, pt: (layer, pt[b, s * pages + i], 0, 0))

    grid_spec = pltpu.PrefetchScalarGridSpec(
        num_scalar_prefetch=1,
        grid=(nb, n_steps),
        in_specs=[pl.BlockSpec((None, rows, hw), lambda b, s, pt: (b, 0, 0)),
                  pl.BlockSpec((None, tk * DIFF_HEADS, hw), lambda b, s, pt: (b, 0, 0)),
                  pl.BlockSpec((None, tk * DIFF_HEADS, hw), lambda b, s, pt: (b, 0, 0)),
                  pl.BlockSpec((None, rows, pages * page_rows),
                               lambda b, s, pt: (jnp.where(s == n_steps - 1, 1, 0), 0, 0)),
                  pl.BlockSpec((rows, tk * DIFF_HEADS), lambda b, s, pt: (0, 0)),
                  pl.BlockSpec((4, hd), lambda b, s, pt: (0, 0)),
                  pl.BlockSpec((1, hw), lambda b, s, pt: (0, 0))]
                 + [page_spec(i) for i in range(pages)] + [page_spec(i) for i in range(pages)],
        out_specs=pl.BlockSpec((None, rows // 2, hw), lambda b, s, pt: (b, 0, 0)),
        scratch_shapes=[pltpu.VMEM((rows, LANES), F32), pltpu.VMEM((rows, LANES), F32),
                        pltpu.VMEM((rows, hw), F32)],
    )
    out = pl.pallas_call(
        functools.partial(_paged_kernel, pages=pages, q_tok=tq, lam_init=lam_init, scale=hd ** -0.5),
        grid_spec=grid_spec,
        out_shape=jax.ShapeDtypeStruct((nb, rows // 2, hw), BF16),
        compiler_params=_params(("arbitrary", "arbitrary"), 56),
        name="paged_diff",
    )(page_table, qd, k_rows, v_rows, tab, tab_new, lam_params, norm_g.reshape(1, hw),
      *([ck] * pages), *([cv] * pages))
    return out.reshape(nb, DIFF_HEADS, tq, hw).transpose(0, 2, 1, 3).reshape(nb, tq, d)


def kernel(x_prompt, x_sample, state_gla, cache_k, cache_v, page_table, ln_g, ln_b, ffn_w_gate, ffn_w_up,
           ffn_w_down, gla_w_in, gla_w_a2, gla_b_a, gla_norm_g, gla_w_o, diff_w_in, diff_lambda, diff_norm_g,
           diff_w_o, rel_bias):
    nb, t, d = x_prompt.shape
    sb, st, _ = x_sample.shape
    hw = d // DIFF_HEADS
    assert st <= SAMPLE_KEY_ROWS <= SAMPLE_ROWS
    xs = jnp.pad(x_sample, ((0, 0), (0, SAMPLE_ROWS - st), (0, 0))).reshape(sb * SAMPLE_ROWS, d)
    x0 = (x_prompt.reshape(nb * t, d), xs)
    xf = ("plain", x0)
    xb = tuple(_to_bf16(a) for a in x0)
    gla_p, gla_s, k_rows, v_rows = [], [], [], []

    def layer_norm(z, g, b):
        outs = [_layer_norm(a, g, b) for a in z]
        xb, mean, rstd = (tuple(o[i] for o in outs) for i in range(3))
        return ("ln", z, mean, rstd, g, b), xb

    def ffn(xf, xb, i, s):
        h = _ffn_up(xb, ffn_w_gate, ffn_w_up, (i, s))
        z = _res_mm(h, ffn_w_down, (i, s), xf, 0.5)
        if (i, s) == (DEPTH - 1, 1):
            return z, None
        return layer_norm(z, ln_g[i, 2 * s], ln_b[i, 2 * s])

    def sample_seq(a, rows):
        return a.reshape(sb, SAMPLE_ROWS, a.shape[-1])[:, :rows]

    def sample_rows(a):
        return jnp.pad(a, ((0, 0), (0, SAMPLE_ROWS - a.shape[1]), (0, 0))).reshape(sb * SAMPLE_ROWS, a.shape[-1])

    for i in range(DEPTH):
        xf, xb = ffn(xf, xb, i, 0)
        j = i // N_MIXERS
        if i % N_MIXERS == 0:
            hk = gla_w_a2.shape[-1]
            hv = gla_w_o.shape[1]
            qkvr = _proj(xb, gla_w_in, (j,), 0, 2 * hk + 2 * hv, 512)
            w_g = jnp.pad(gla_w_in[j][:, 2 * hk + 2 * hv:], ((0, 0), (0, LANES - GLA_GATE_RANK)))
            glr = _proj(xb, w_g[None], (0,), 0, LANES, LANES)
            w_a2p = jnp.pad(gla_w_a2[j], ((0, LANES - GLA_GATE_RANK), (0, 0)))
            o_p, s_p = _gla_core(qkvr[0], glr[0], w_a2p, gla_b_a[j], gla_norm_g[j], None, nb, t)
            chunk = lambda a: jnp.pad(sample_seq(a, SAMPLE_ROWS), ((0, 0), (0, GLA_CHUNK - SAMPLE_ROWS), (0, 0))
                                      ).reshape(sb * GLA_CHUNK, a.shape[-1])
            o_s, s_s = _gla_core(chunk(qkvr[1]), chunk(glr[1]), w_a2p, gla_b_a[j], gla_norm_g[j],
                                 state_gla[j], sb, st)
            o_s = o_s.reshape(sb, GLA_CHUNK, hv)[:, :SAMPLE_ROWS].reshape(sb * SAMPLE_ROWS, hv)
            gla_p.append(s_p)
            gla_s.append(s_s)
            z = _res_mm((o_p, o_s), gla_w_o, (j,), xf, 1.0)
        else:
            lam_init = 0.8 - 0.6 * math.exp(-0.3 * i)
            q = _proj(xb, diff_w_in, (j,), 0, d, 512)
            k = _proj(xb, diff_w_in, (j,), d // 512, d, 512)
            v = _proj(xb, diff_w_in, (j,), 2 * d // 512, d, 512)
            k_rows.append(k)
            v_rows.append(v)
            o_p = _flash_diff(q[0], k[0], v[0], rel_bias, diff_lambda[j], diff_norm_g[j], nb, lam_init)
            o_s = _paged_diff(sample_seq(q[1], st), sample_seq(k[1], SAMPLE_KEY_ROWS),
                              sample_seq(v[1], SAMPLE_KEY_ROWS), cache_k, cache_v, page_table, j, rel_bias,
                              diff_lambda[j], diff_norm_g[j], lam_init)
            z = _res_mm((o_p, sample_rows(o_s)), diff_w_o, (j,), xf, 1.0)
        xf, xb = layer_norm(z, ln_g[i, 1], ln_b[i, 1])
        xf, xb = ffn(xf, xb, i, 1)

    y = [_layer_norm(a, ln_g[DEPTH - 1, 2], ln_b[DEPTH - 1, 2], final=True)[0] for a in xf]
    heads = lambda a, n_seq, rows: a.reshape(n_seq, rows, DIFF_HEADS, hw)
    return (y[0].reshape(nb, t, d),
            sample_seq(y[1], st),
            jnp.stack(gla_p),
            jnp.stack(gla_s),
            jnp.stack([heads(k[0], nb, t) for k in k_rows]),
            jnp.stack([heads(v[0], nb, t) for v in v_rows]),
            jnp.stack([heads(sample_seq(k[1], st), sb, st) for k in k_rows]),
            jnp.stack([heads(sample_seq(v[1], st), sb, st) for v in v_rows]))
```

```python
import functools
import math

import jax
import jax.numpy as jnp
import numpy as np
from jax import lax
from jax.experimental import pallas as pl
from jax.experimental.pallas import tpu as pltpu

F32 = jnp.float32
BF16 = jnp.bfloat16

DEPTH = 2
N_MIXERS = 2
GLA_HEADS = 4
GLA_GATE_RANK = 16
GLA_TAU = 16.0
GLA_CHUNK = 64
DIFF_HEADS = 16
PAGE_SIZE = 128
REL_BUCKETS = 32
REL_MAX_DIST = 128
DEEPNORM_ALPHA = (2.0 * DEPTH) ** 0.25
LN_EPS = 1e-5
NEG = -1e30
LOG2E = 1.0 / math.log(2.0)

LANES = 128
MIB = 1024 * 1024
SPLIT_K_ABOVE = 8192
FLASH_TILE = 512
PAGES_PER_STEP = 4
SAMPLE_ROWS = 16
SAMPLE_KEY_ROWS = 8


def _row_tile(m, cap):
    for t in range(min(m, cap) // 16 * 16, 15, -16):
        if m % t == 0:
            return t
    raise ValueError(f"no row tile for {m} rows")


def _params(sem, vmem_mib):
    return pltpu.CompilerParams(dimension_semantics=sem, vmem_limit_bytes=vmem_mib * MIB)


def _ln_apply(z, mean, rstd, g, b):
    return (z - mean) * rstd * g + b


def _ln_kernel(z_ref, g_ref, b_ref, *out_refs, final):
    z = z_ref[...]
    mean = jnp.mean(z, axis=-1, keepdims=True)
    zc = z - mean
    rstd = lax.rsqrt(jnp.mean(zc * zc, axis=-1, keepdims=True) + LN_EPS)
    y = _ln_apply(z, mean, rstd, g_ref[...], b_ref[...])
    if final:
        out_refs[0][...] = y
    else:
        xb_ref, mean_ref, rstd_ref = out_refs
        xb_ref[...] = y.astype(BF16)
        mean_ref[...] = jnp.broadcast_to(mean, mean_ref.shape)
        rstd_ref[...] = jnp.broadcast_to(rstd, rstd_ref.shape)


def _layer_norm(z, g, b, final=False):
    m, d = z.shape
    tr = _row_tile(m, 256)
    rows = lambda w: pl.BlockSpec((tr, w), lambda i: (i, 0))
    if final:
        out_specs, out_shape = [rows(d)], [jax.ShapeDtypeStruct((m, d), F32)]
    else:
        out_specs = [rows(d), rows(LANES), rows(LANES)]
        out_shape = [jax.ShapeDtypeStruct((m, d), BF16), jax.ShapeDtypeStruct((m, LANES), F32),
                     jax.ShapeDtypeStruct((m, LANES), F32)]
    return pl.pallas_call(
        functools.partial(_ln_kernel, final=final),
        grid=(m // tr,),
        in_specs=[rows(d), pl.BlockSpec((1, d), lambda i: (0, 0)), pl.BlockSpec((1, d), lambda i: (0, 0))],
        out_specs=out_specs,
        out_shape=out_shape,
        compiler_params=_params(("arbitrary",), 40),
        name="layer_norm",
    )(z, g.reshape(1, d), b.reshape(1, d))


def _cast_kernel(x_ref, o_ref):
    o_ref[...] = x_ref[...].astype(BF16)


def _to_bf16(x):
    m, d = x.shape
    tr = _row_tile(m, 256)
    return pl.pallas_call(
        _cast_kernel,
        grid=(m // tr,),
        in_specs=[pl.BlockSpec((tr, d), lambda i: (i, 0))],
        out_specs=pl.BlockSpec((tr, d), lambda i: (i, 0)),
        out_shape=jax.ShapeDtypeStruct((m, d), BF16),
        compiler_params=_params(("arbitrary",), 40),
        name="to_bf16",
    )(x)


def _w_spec(w, lead, k, tn, col0):
    none = (None,) * len(lead)
    return pl.BlockSpec(none + (k, tn), lambda n, m: tuple(lead) + (0, col0 + n))


def _row_specs(tm, ms, width, col):
    return [pl.BlockSpec((tm, width), lambda n, i: (i, col(n))),
            pl.BlockSpec((ms, width), lambda n, i: (0, col(n)))]


def _pair_shapes(m, ms, n_cols, dtype):
    return [jax.ShapeDtypeStruct((m, n_cols), dtype), jax.ShapeDtypeStruct((ms, n_cols), dtype)]


def _proj_kernel(x_ref, xs_ref, w_ref, o_ref, os_ref, wb_ref):
    @pl.when(pl.program_id(1) == 0)
    def _():
        wb_ref[...] = w_ref[...].astype(BF16)
        os_ref[...] = jnp.dot(xs_ref[...], wb_ref[...], preferred_element_type=F32).astype(os_ref.dtype)

    o_ref[...] = jnp.dot(x_ref[...], wb_ref[...], preferred_element_type=F32).astype(o_ref.dtype)


def _proj(x, w, lead, col0, n_cols, tn, out_dtype=F32):
    (m, k), ms = x[0].shape, x[1].shape[0]
    tm = _row_tile(m, 1024)
    return pl.pallas_call(
        _proj_kernel,
        grid=(n_cols // tn, m // tm),
        in_specs=_row_specs(tm, ms, k, lambda n: 0) + [_w_spec(w, lead, k, tn, col0)],
        out_specs=_row_specs(tm, ms, tn, lambda n: n),
        out_shape=_pair_shapes(m, ms, n_cols, out_dtype),
        scratch_shapes=[pltpu.VMEM((k, tn), BF16)],
        compiler_params=_params(("arbitrary", "arbitrary"), 56),
        name="proj",
    )(x[0], x[1], w)


def _ffn_up_kernel(x_ref, xs_ref, wg_ref, wu_ref, h_ref, hs_ref, wgb_ref, wub_ref):
    def swiglu(x):
        g = jnp.dot(x, wgb_ref[...], preferred_element_type=F32)
        u = jnp.dot(x, wub_ref[...], preferred_element_type=F32)
        return (g * jax.nn.sigmoid(g) * u).astype(BF16)

    @pl.when(pl.program_id(1) == 0)
    def _():
        wgb_ref[...] = wg_ref[...].astype(BF16)
        wub_ref[...] = wu_ref[...].astype(BF16)
        hs_ref[...] = swiglu(xs_ref[...])

    h_ref[...] = swiglu(x_ref[...])


def _ffn_up(x, w_gate, w_up, lead):
    (m, k), ms = x[0].shape, x[1].shape[0]
    f = w_gate.shape[-1]
    tm = _row_tile(m, 1024)
    tn = 256
    return pl.pallas_call(
        _ffn_up_kernel,
        grid=(f // tn, m // tm),
        in_specs=_row_specs(tm, ms, k, lambda n: 0)
                 + [_w_spec(w_gate, lead, k, tn, 0), _w_spec(w_up, lead, k, tn, 0)],
        out_specs=_row_specs(tm, ms, tn, lambda n: n),
        out_shape=_pair_shapes(m, ms, f, BF16),
        scratch_shapes=[pltpu.VMEM((k, tn), BF16), pltpu.VMEM((k, tn), BF16)],
        compiler_params=_params(("arbitrary", "arbitrary"), 56),
        name="ffn_up",
    )(x[0], x[1], w_gate, w_up)


def _res_mm_kernel(*refs, alpha, scale, has_partial, res_mode):
    h_ref, hs_ref, w_ref = refs[:3]
    o_ref, os_ref, wb_ref = refs[-3:]
    extra = list(refs[3:-3])
    partial = [extra.pop(0), extra.pop(0)] if has_partial else None

    def residual(group):
        if res_mode == "plain":
            return extra[group][...]
        z, mean, rstd = extra[group], extra[2 + group], extra[4 + group]
        wide = lambda a: pltpu.repeat(a[...], z.shape[1] // LANES, axis=1)
        return _ln_apply(z[...], wide(mean), wide(rstd), extra[6][...], extra[7][...])

    def epilogue(y, group):
        if has_partial:
            y = partial[group][...] + y
        if res_mode is not None:
            y = alpha * residual(group) + scale * y
        return y

    @pl.when(pl.program_id(1) == 0)
    def _():
        wb_ref[...] = w_ref[...].astype(BF16)
        os_ref[...] = epilogue(jnp.dot(hs_ref[...], wb_ref[...], preferred_element_type=F32), 1)

    o_ref[...] = epilogue(jnp.dot(h_ref[...], wb_ref[...], preferred_element_type=F32), 0)


def _res_mm_pass(h, w, lead, kb, n_kb, partial, res, scale, tm_cap):
    (m, k), ms = h[0].shape, h[1].shape[0]
    kh = k // n_kb
    n_cols = w.shape[-1]
    tn = 512
    tm = _row_tile(m, tm_cap)
    none = (None,) * len(lead)
    tiles = lambda: _row_specs(tm, ms, tn, lambda n: n)
    in_specs = _row_specs(tm, ms, kh, lambda n: kb) + [
        pl.BlockSpec(none + (kh, tn), lambda n, i: tuple(lead) + (kb, n))]
    args = [h[0], h[1], w]
    if partial is not None:
        in_specs += tiles()
        args += list(partial)
    if res is not None and res[0] == "plain":
        in_specs += tiles()
        args += list(res[1])
    elif res is not None:
        _, z, mean, rstd, g, b = res
        vec = pl.BlockSpec((1, tn), lambda n, i: (0, n))
        in_specs += tiles() + 2 * _row_specs(tm, ms, LANES, lambda n: 0) + [vec, vec]
        args += list(z) + list(mean) + list(rstd) + [g.reshape(1, n_cols), b.reshape(1, n_cols)]
    return pl.pallas_call(
        functools.partial(_res_mm_kernel, alpha=DEEPNORM_ALPHA, scale=scale,
                          has_partial=partial is not None, res_mode=None if res is None else res[0]),
        grid=(n_cols // tn, m // tm),
        in_specs=in_specs,
        out_specs=_row_specs(tm, ms, tn, lambda n: n),
        out_shape=_pair_shapes(m, ms, n_cols, F32),
        scratch_shapes=[pltpu.VMEM((kh, tn), BF16)],
        compiler_params=_params(("arbitrary", "arbitrary"), 56),
        name="res_mm",
    )(*args)


def _res_mm(h, w, lead, res, scale):
    k = h[0].shape[1]
    if k > SPLIT_K_ABOVE and (k // 2) % LANES == 0:
        part = _res_mm_pass(h, w, lead, 0, 2, None, None, scale, 512)
        return _res_mm_pass(h, w, lead, 1, 2, part, res, scale, 512)
    return _res_mm_pass(h, w, lead, 0, 1, None, res, scale, 1024)


def _log_sigmoid(x):
    return jnp.minimum(x, 0.0) - jnp.log(1.0 + jnp.exp(-jnp.abs(x)))


def _gla_kernel(*refs, chunk, valid, has_s0, q_scale, dk, dv):
    if has_s0:
        (q_ref, k_ref, v_ref, r_ref, glr_ref, wa2_ref, ba_ref, ng_ref, s0_ref,
         o_ref, sfin_ref, s_ref) = refs
    else:
        (q_ref, k_ref, v_ref, r_ref, glr_ref, wa2_ref, ba_ref, ng_ref,
         o_ref, sfin_ref, s_ref) = refs
    c = pl.program_id(1)

    @pl.when(c == 0)
    def _():
        if has_s0:
            s_ref[...] = s0_ref[...]
        else:
            s_ref[...] = jnp.zeros_like(s_ref)

    hk = q_ref.shape[-1]
    pre = jnp.dot(glr_ref[...].astype(BF16), wa2_ref[...].astype(BF16),
                  preferred_element_type=F32) + ba_ref[...]
    log_a = _log_sigmoid(pre) / GLA_TAU
    k = k_ref[...]
    if valid < chunk:
        row_ok = lax.broadcasted_iota(jnp.int32, (chunk, 1), 0) < valid
        log_a = jnp.where(row_ok, log_a, 0.0)
        k = jnp.where(row_ok, k, 0.0)
    ri = lax.broadcasted_iota(jnp.int32, (chunk, chunk), 0)
    ci = lax.broadcasted_iota(jnp.int32, (chunk, chunk), 1)
    causal = ri >= ci
    b = jnp.dot(jnp.where(causal, 1.0, 0.0), log_a, precision=lax.Precision.HIGHEST,
                preferred_element_type=F32)
    b_last = b[chunk - 1:chunk, :]
    q_t = (q_ref[...] * q_scale * jnp.exp(b)).astype(BF16)
    k_t = (k * jnp.exp(-b)).astype(BF16)
    k_end = (k * jnp.exp(b_last - b)).astype(BF16)
    decay_col = jnp.exp(jnp.broadcast_to(b_last, (LANES, hk)).T[:, :1])
    last = c == pl.num_programs(1) - 1
    for h in range(GLA_HEADS):
        ks = slice(h * dk, (h + 1) * dk)
        vs = slice(h * dv, (h + 1) * dv)
        v = v_ref[:, vs].astype(BF16)
        att = lax.dot_general(q_t[:, ks], k_t[:, ks], (((1,), (1,)), ((), ())),
                              preferred_element_type=F32)
        att = jnp.where(causal, att, 0.0).astype(BF16)
        s_old = s_ref[h]
        o = (jnp.dot(att, v, preferred_element_type=F32)
             + jnp.dot(q_t[:, ks], s_old.astype(BF16), preferred_element_type=F32))
        kv = lax.dot_general(k_end[:, ks], v, (((0,), (0,)), ((), ())), preferred_element_type=F32)
        s_new = decay_col[ks] * s_old + kv
        s_ref[h] = s_new

        @pl.when(last)
        def _(h=h, s_new=s_new):
            sfin_ref[h] = s_new

        oc = o - jnp.mean(o, axis=-1, keepdims=True)
        var = jnp.mean(oc * oc, axis=-1, keepdims=True)
        on = oc * lax.rsqrt(var + LN_EPS) * ng_ref[...]
        r = r_ref[:, vs]
        o_ref[:, vs] = (on * (r * jax.nn.sigmoid(r))).astype(BF16)


def _gla_core(qkvr, glr, w_a2p, b_a, norm_g, s0, n_seq, valid):
    m = qkvr.shape[0]
    hk = w_a2p.shape[-1]
    dk = hk // GLA_HEADS
    dv = norm_g.shape[-1]
    t = m // n_seq
    chunk = GLA_CHUNK if t % GLA_CHUNK == 0 else t
    nc = t // chunk
    hv = GLA_HEADS * dv
    assert (2 * hk) % hv == 0
    vb0 = 2 * hk // hv
    has_s0 = s0 is not None
    row = lambda b, c: b * nc + c
    state = pl.BlockSpec((None, GLA_HEADS, dk, dv), lambda b, c: (b, 0, 0, 0))
    in_specs = [
        pl.BlockSpec((chunk, hk), lambda b, c: (row(b, c), 0)),
        pl.BlockSpec((chunk, hk), lambda b, c: (row(b, c), 1)),
        pl.BlockSpec((chunk, hv), lambda b, c: (row(b, c), vb0)),
        pl.BlockSpec((chunk, hv), lambda b, c: (row(b, c), vb0 + 1)),
        pl.BlockSpec((chunk, LANES), lambda b, c: (row(b, c), 0)),
        pl.BlockSpec((LANES, hk), lambda b, c: (0, 0)),
        pl.BlockSpec((1, hk), lambda b, c: (0, 0)),
        pl.BlockSpec((1, dv), lambda b, c: (0, 0)),
    ]
    args = [qkvr, qkvr, qkvr, qkvr, glr, w_a2p, b_a.reshape(1, hk), norm_g.reshape(1, dv)]
    if has_s0:
        in_specs.append(state)
        args.append(s0)
    return pl.pallas_call(
        functools.partial(_gla_kernel, chunk=chunk, valid=valid, has_s0=has_s0, q_scale=dk ** -0.5,
                          dk=dk, dv=dv),
        grid=(n_seq, nc),
        in_specs=in_specs,
        out_specs=[pl.BlockSpec((chunk, hv), lambda b, c: (row(b, c), 0)), state],
        out_shape=[jax.ShapeDtypeStruct((m, hv), BF16),
                   jax.ShapeDtypeStruct((n_seq, GLA_HEADS, dk, dv), F32)],
        scratch_shapes=[pltpu.VMEM((GLA_HEADS, dk, dv), F32)],
        compiler_params=_params(("arbitrary", "arbitrary"), 56),
        name="gla_core",
    )(*args)


def _t5_bucket(dist):
    n = jnp.maximum(dist, 0)
    max_exact = REL_BUCKETS // 2
    large = max_exact + (jnp.log(jnp.maximum(n, 1).astype(F32) / max_exact)
                         / math.log(REL_MAX_DIST / max_exact) * (REL_BUCKETS - max_exact)).astype(jnp.int32)
    large = jnp.minimum(large, REL_BUCKETS - 1)
    return jnp.where(n < max_exact, n, large)


def _bias_kernel(qpos_ref, kpos_ref, qgrp_ref, kgrp_ref, rb_ref, o_ref):
    bucket = _t5_bucket(qpos_ref[...] - kpos_ref[...])
    rb = rb_ref[...]
    acc = jnp.zeros(o_ref.shape, F32)
    for i in range(REL_BUCKETS):
        acc = jnp.where(bucket == i, rb[:, i:i + 1], acc)
    o_ref[...] = jnp.where(qgrp_ref[...] == kgrp_ref[...], acc, NEG)


def _bias_table(qpos, kpos, rb_rows, tr, tc, qgrp, kgrp):
    r, c = qpos.shape[0], kpos.shape[0]
    col = lambda a: a.reshape(r, 1).astype(jnp.int32)
    row = lambda a: a.reshape(1, c).astype(jnp.int32)
    return pl.pallas_call(
        _bias_kernel,
        grid=(r // tr, c // tc),
        in_specs=[pl.BlockSpec((tr, 1), lambda i, j: (i, 0)),
                  pl.BlockSpec((1, tc), lambda i, j: (0, j)),
                  pl.BlockSpec((tr, 1), lambda i, j: (i, 0)),
                  pl.BlockSpec((1, tc), lambda i, j: (0, j)),
                  pl.BlockSpec((tr, REL_BUCKETS), lambda i, j: (i, 0))],
        out_specs=pl.BlockSpec((tr, tc), lambda i, j: (i, j)),
        out_shape=jax.ShapeDtypeStruct((r, c), F32),
        compiler_params=_params(("arbitrary", "arbitrary"), 32),
        name="t5_bias",
    )(col(qpos), row(kpos), col(qgrp), row(kgrp), rb_rows)


def _toeplitz_bias_kernel(rb_ref, o_ref, *, t, mult):
    off = pl.program_id(1)
    w = 2 * t
    u = lax.broadcasted_iota(jnp.int32, (1, w), 1)
    dist = off * t + jnp.where(u < t, -u, w - u)
    bucket = _t5_bucket(dist)
    rb = rb_ref[...]
    g = jnp.zeros((1, w), F32)
    for i in range(REL_BUCKETS):
        g = jnp.where(bucket == i, rb[:, i:i + 1], g)
    x = jnp.broadcast_to(g * mult, (t, w))
    o_ref[...] = pltpu.roll(x, 0, 1, stride=1, stride_axis=0)[:, :t]


def _toeplitz_bias(rel_bias, n_off, t, mult):
    n_heads = rel_bias.shape[1]
    return pl.pallas_call(
        functools.partial(_toeplitz_bias_kernel, t=t, mult=mult),
        grid=(n_heads, n_off),
        in_specs=[pl.BlockSpec((None, 1, REL_BUCKETS), lambda h, o: (h, 0, 0))],
        out_specs=pl.BlockSpec((None, None, t, t), lambda h, o: (h, o, 0, 0)),
        out_shape=jax.ShapeDtypeStruct((n_heads, n_off, t, t), F32),
        compiler_params=_params(("arbitrary", "arbitrary"), 32),
        name="t5_bias_tiles",
    )(rel_bias.T.reshape(n_heads, 1, REL_BUCKETS))


def _first_const_dist(limit):
    n = np.arange(1, limit + 1)
    large = 16 + (np.log(n.astype(np.float32) / np.float32(16)) / np.float32(math.log(8.0))
                  * np.float32(16)).astype(np.int32)
    bucket = np.where(n < 16, n, np.minimum(large, REL_BUCKETS - 1))
    not_last = np.nonzero(bucket != REL_BUCKETS - 1)[0]
    return int(n[not_last[-1]] + 1) if not_last.size else 1


def _lambda(lam_ref, lam_init):
    lp = lam_ref[...]
    a = jnp.sum(lp[0:1] * lp[1:2], axis=-1, keepdims=True)
    b = jnp.sum(lp[2:3] * lp[3:4], axis=-1, keepdims=True)
    return jnp.exp(a) - jnp.exp(b) + lam_init


def _flash_kernel(q_ref, k_ref, v_ref, bias_ref, lam_ref, ng_ref, o_ref,
                  kb_ref, vb_ref, m_ref, l_ref, acc_ref, *, t, hd, n_near, lam_init, scale):
    qi = pl.program_id(2)

    @pl.when(qi == 0)
    def _():
        kb_ref[...] = k_ref[...].astype(BF16)
        vb_ref[...] = v_ref[...].astype(BF16)

    q = q_ref[...] * (scale * LOG2E)
    lane = lax.broadcasted_iota(jnp.int32, q.shape, 1)
    q2 = jnp.concatenate([jnp.where(lane < hd, q, 0.0), jnp.where(lane >= hd, q, 0.0)],
                         axis=0).astype(BF16)
    m_ref[...] = jnp.full(m_ref.shape, NEG, F32)
    l_ref[...] = jnp.zeros(l_ref.shape, F32)
    acc_ref[...] = jnp.zeros(acc_ref.shape, F32)
    far_bias = bias_ref[n_near - 1, t - 1:t, 0:1]

    def tile(kj, bias, mask):
        rows = pl.ds(pl.multiple_of(kj * t, t), t)
        kt = kb_ref[rows, :]
        vt = vb_ref[rows, :]
        for c in range(2):
            rc = slice(c * t, (c + 1) * t)
            s = lax.dot_general(q2[rc], kt, (((1,), (1,)), ((), ())),
                                preferred_element_type=F32)
            s = s + bias
            if mask is not None:
                s = jnp.where(mask, s, NEG)
            m_old = m_ref[rc, :]
            m_new = jnp.maximum(m_old, jnp.max(s, axis=-1, keepdims=True))
            alpha = jnp.exp2(m_old - m_new)
            p = jnp.exp2(s - pltpu.repeat(m_new, t // LANES, axis=1))
            l_ref[rc, :] = alpha * l_ref[rc, :] + jnp.sum(p, axis=-1, keepdims=True)
            acc_ref[rc, :] = (pltpu.repeat(alpha, 2 * hd // LANES, axis=1) * acc_ref[rc, :]
                              + jnp.dot(p.astype(BF16), vt, preferred_element_type=F32))
            m_ref[rc, :] = m_new

    def far_body(kj, carry):
        tile(kj, far_bias, None)
        return carry

    lax.fori_loop(0, jnp.maximum(qi - (n_near - 1), 0), far_body, 0)
    for off in range(n_near - 1, 0, -1):
        @pl.when(qi >= off)
        def _(off=off):
            tile(qi - off, bias_ref[off], None)
    ri = lax.broadcasted_iota(jnp.int32, (t, t), 0)
    ci = lax.broadcasted_iota(jnp.int32, (t, t), 1)
    tile(qi, bias_ref[0], ri >= ci)

    lam = _lambda(lam_ref, lam_init)
    a = acc_ref[...] / pltpu.repeat(l_ref[...], 2 * hd // LANES, axis=1)
    o = a[:t] - lam * a[t:]
    y = o * lax.rsqrt(jnp.mean(o * o, axis=-1, keepdims=True) + LN_EPS) * ng_ref[...]
    o_ref[...] = (y * (1.0 - lam_init)).astype(BF16)


def _flash_diff(qkv, rel_bias, lam_params, norm_g, n_seq, lam_init):
    m, d = qkv.shape[0], qkv.shape[1] // 3
    t_seq = m // n_seq
    hw = d // DIFF_HEADS
    hd = hw // 2
    t = next(c for c in range(FLASH_TILE, 0, -LANES) if t_seq % c == 0)
    nq = t_seq // t
    n_near = min(nq, -(-(_first_const_dist(t_seq) - 1) // t) + 1)
    bias = _toeplitz_bias(rel_bias, n_near, t, LOG2E)
    return pl.pallas_call(
        functools.partial(_flash_kernel, t=t, hd=hd, n_near=n_near, lam_init=lam_init, scale=hd ** -0.5),
        grid=(n_seq, DIFF_HEADS, nq),
        in_specs=[pl.BlockSpec((t, hw), lambda b, h, i: (b * nq + i, h)),
                  pl.BlockSpec((t_seq, hw), lambda b, h, i: (b, DIFF_HEADS + h)),
                  pl.BlockSpec((t_seq, hw), lambda b, h, i: (b, 2 * DIFF_HEADS + h)),
                  pl.BlockSpec((None, n_near, t, t), lambda b, h, i: (h, 0, 0, 0)),
                  pl.BlockSpec((4, hd), lambda b, h, i: (0, 0)),
                  pl.BlockSpec((1, hw), lambda b, h, i: (0, 0))],
        out_specs=pl.BlockSpec((t, hw), lambda b, h, i: (b * nq + i, h)),
        out_shape=jax.ShapeDtypeStruct((m, d), BF16),
        scratch_shapes=[pltpu.VMEM((t_seq, hw), BF16), pltpu.VMEM((t_seq, hw), BF16),
                        pltpu.VMEM((2 * t, LANES), F32), pltpu.VMEM((2 * t, LANES), F32),
                        pltpu.VMEM((2 * t, hw), F32)],
        compiler_params=_params(("arbitrary", "arbitrary", "arbitrary"), 48),
        name="flash_diff",
    )(qkv, qkv, qkv, bias, lam_params, norm_g.reshape(1, hw))


def _paged_kernel(pt_ref, q_ref, kn_ref, vn_ref, tab_ref, tabn_ref, lam_ref, ng_ref, *rest,
                  pages, q_tok, lam_init, scale):
    k_refs = rest[:pages]
    v_refs = rest[pages:2 * pages]
    o_ref, m_ref, l_ref, acc_ref = rest[2 * pages:]
    s_id = pl.program_id(1)
    rows, hw = q_ref.shape
    half = rows // 2
    q = q_ref[...]

    def scores(kmat):
        return lax.dot_general(q, kmat, (((1,), (1,)), ((), ())), preferred_element_type=F32) * scale

    def update(s, vmat):
        m_old = m_ref[...]
        m_new = jnp.maximum(m_old, jnp.max(s, axis=-1, keepdims=True))
        alpha = jnp.exp(m_old - m_new)
        p = jnp.exp(s - pltpu.repeat(m_new, s.shape[1] // LANES, axis=1))
        l_ref[...] = alpha * l_ref[...] + jnp.sum(p, axis=-1, keepdims=True)
        acc_ref[...] = (pltpu.repeat(alpha, hw // LANES, axis=1) * acc_ref[...]
                        + jnp.dot(p.astype(BF16), vmat, preferred_element_type=F32))
        m_ref[...] = m_new

    @pl.when(s_id == 0)
    def _():
        m_ref[...] = jnp.full(m_ref.shape, NEG, F32)
        l_ref[...] = jnp.zeros(l_ref.shape, F32)
        acc_ref[...] = jnp.zeros(acc_ref.shape, F32)
        s = scores(kn_ref[...].astype(BF16)) + tabn_ref[...]
        tq = lax.broadcasted_iota(jnp.int32, s.shape, 0) % q_tok
        tk = lax.broadcasted_iota(jnp.int32, s.shape, 1) // DIFF_HEADS
        update(jnp.where(tk <= tq, s, NEG), vn_ref[...].astype(BF16))

    kp = jnp.concatenate([r[...].astype(BF16) for r in k_refs], axis=0)
    vp = jnp.concatenate([r[...].astype(BF16) for r in v_refs], axis=0)
    update(scores(kp) + tab_ref[...], vp)

    @pl.when(s_id == pl.num_programs(1) - 1)
    def _():
        a = acc_ref[...] / pltpu.repeat(l_ref[...], hw // LANES, axis=1)
        lam = _lambda(lam_ref, lam_init)
        o = a[:half] - lam * a[half:]
        y = o * lax.rsqrt(jnp.mean(o * o, axis=-1, keepdims=True) + LN_EPS) * ng_ref[...]
        o_ref[...] = (y * (1.0 - lam_init)).astype(o_ref.dtype)


def _paged_diff(q, k_new, v_new, cache_k, cache_v, page_table, layer, rel_bias, lam_params, norm_g,
                lam_init):
    nb, tq, d = q.shape
    tk = k_new.shape[1]
    assert tq <= tk
    hw = d // DIFF_HEADS
    hd = hw // 2
    n_pages = page_table.shape[1]
    past = n_pages * PAGE_SIZE
    pages = next(p for p in (PAGES_PER_STEP, 2, 1) if n_pages % p == 0)
    n_steps = n_pages // pages
    rows = 2 * DIFF_HEADS * tq
    page_rows = PAGE_SIZE * DIFF_HEADS
    assert pages * PAGE_SIZE + 1 >= _first_const_dist(past + tq)
    q5 = q.reshape(nb, tq, DIFF_HEADS, 2, hd).transpose(0, 3, 2, 1, 4)
    same_c = jnp.arange(2)[:, None] == jnp.arange(2)[None, :]
    qd = jnp.where(same_c[None, :, None, None, :, None], q5[:, :, :, :, None, :], 0.0)
    qd = qd.reshape(nb, rows, hw).astype(BF16)
    k_rows = k_new.reshape(nb, tk * DIFF_HEADS, hw)
    v_rows = v_new.reshape(nb, tk * DIFF_HEADS, hw)
    qpos = jnp.tile(past + jnp.arange(tq), 2 * DIFF_HEADS)
    qgrp = jnp.tile(jnp.repeat(jnp.arange(DIFF_HEADS), tq), 2)
    rb_rows = jnp.tile(jnp.repeat(rel_bias.T, tq, axis=0), (2, 1))
    col_tok = jnp.repeat(jnp.arange(pages * PAGE_SIZE), DIFF_HEADS)
    col_grp = jnp.tile(jnp.arange(DIFF_HEADS), pages * PAGE_SIZE)
    kpos = jnp.concatenate([jnp.zeros_like(col_tok), (n_steps - 1) * pages * PAGE_SIZE + col_tok])
    tab = _bias_table(qpos, kpos, rb_rows, rows, page_rows, qgrp, jnp.tile(col_grp, 2))
    tab = tab.reshape(rows, 2, pages * page_rows).transpose(1, 0, 2)
    tab_new = _bias_table(qpos, past + jnp.repeat(jnp.arange(tk), DIFF_HEADS), rb_rows, rows,
                          tk * DIFF_HEADS, qgrp, jnp.tile(jnp.arange(DIFF_HEADS), tk))
    ck = cache_k.reshape(cache_k.shape[0], cache_k.shape[1], page_rows, hw)
    cv = cache_v.reshape(cache_v.shape[0], cache_v.shape[1], page_rows, hw)

    def page_spec(i):
        return pl.BlockSpec((None, None, page_rows, hw),
                            lambda b, s, pt: (layer, pt[b, s * pages + i], 0, 0))

    grid_spec = pltpu.PrefetchScalarGridSpec(
        num_scalar_prefetch=1,
        grid=(nb, n_steps),
        in_specs=[pl.BlockSpec((None, rows, hw), lambda b, s, pt: (b, 0, 0)),
                  pl.BlockSpec((None, tk * DIFF_HEADS, hw), lambda b, s, pt: (b, 0, 0)),
                  pl.BlockSpec((None, tk * DIFF_HEADS, hw), lambda b, s, pt: (b, 0, 0)),
                  pl.BlockSpec((None, rows, pages * page_rows),
                               lambda b, s, pt: (jnp.where(s == n_steps - 1, 1, 0), 0, 0)),
                  pl.BlockSpec((rows, tk * DIFF_HEADS), lambda b, s, pt: (0, 0)),
                  pl.BlockSpec((4, hd), lambda b, s, pt: (0, 0)),
                  pl.BlockSpec((1, hw), lambda b, s, pt: (0, 0))]
                 + [page_spec(i) for i in range(pages)] + [page_spec(i) for i in range(pages)],
        out_specs=pl.BlockSpec((None, rows // 2, hw), lambda b, s, pt: (b, 0, 0)),
        scratch_shapes=[pltpu.VMEM((rows, LANES), F32), pltpu.VMEM((rows, LANES), F32),
                        pltpu.VMEM((rows, hw), F32)],
    )
    out = pl.pallas_call(
        functools.partial(_paged_kernel, pages=pages, q_tok=tq, lam_init=lam_init, scale=hd ** -0.5),
        grid_spec=grid_spec,
        out_shape=jax.ShapeDtypeStruct((nb, rows // 2, hw), BF16),
        compiler_params=_params(("arbitrary", "arbitrary"), 56),
        name="paged_diff",
    )(page_table, qd, k_rows, v_rows, tab, tab_new, lam_params, norm_g.reshape(1, hw),
      *([ck] * pages), *([cv] * pages))
    return out.reshape(nb, DIFF_HEADS, tq, hw).transpose(0, 2, 1, 3).reshape(nb, tq, d)


def kernel(x_prompt, x_sample, state_gla, cache_k, cache_v, page_table, ln_g, ln_b, ffn_w_gate, ffn_w_up,
           ffn_w_down, gla_w_in, gla_w_a2, gla_b_a, gla_norm_g, gla_w_o, diff_w_in, diff_lambda, diff_norm_g,
           diff_w_o, rel_bias):
    nb, t, d = x_prompt.shape
    sb, st, _ = x_sample.shape
    hw = d // DIFF_HEADS
    assert st <= SAMPLE_KEY_ROWS <= SAMPLE_ROWS
    xs = jnp.pad(x_sample, ((0, 0), (0, SAMPLE_ROWS - st), (0, 0))).reshape(sb * SAMPLE_ROWS, d)
    x0 = (x_prompt.reshape(nb * t, d), xs)
    xf = ("plain", x0)
    xb = tuple(_to_bf16(a) for a in x0)
    gla_p, gla_s, k_rows, v_rows = [], [], [], []

    def layer_norm(z, g, b):
        outs = [_layer_norm(a, g, b) for a in z]
        xb, mean, rstd = (tuple(o[i] for o in outs) for i in range(3))
        return ("ln", z, mean, rstd, g, b), xb

    def ffn(xf, xb, i, s):
        h = _ffn_up(xb, ffn_w_gate, ffn_w_up, (i, s))
        z = _res_mm(h, ffn_w_down, (i, s), xf, 0.5)
        if (i, s) == (DEPTH - 1, 1):
            return z, None
        return layer_norm(z, ln_g[i, 2 * s], ln_b[i, 2 * s])

    def sample_seq(a, rows):
        return a.reshape(sb, SAMPLE_ROWS, a.shape[-1])[:, :rows]

    def sample_rows(a):
        return jnp.pad(a, ((0, 0), (0, SAMPLE_ROWS - a.shape[1]), (0, 0))).reshape(sb * SAMPLE_ROWS, a.shape[-1])

    for i in range(DEPTH):
        xf, xb = ffn(xf, xb, i, 0)
        j = i // N_MIXERS
        if i % N_MIXERS == 0:
            hk = gla_w_a2.shape[-1]
            hv = gla_w_o.shape[1]
            qkvr = _proj(xb, gla_w_in, (j,), 0, 2 * hk + 2 * hv, 512)
            w_g = jnp.pad(gla_w_in[j][:, 2 * hk + 2 * hv:], ((0, 0), (0, LANES - GLA_GATE_RANK)))
            glr = _proj(xb, w_g[None], (0,), 0, LANES, LANES)
            w_a2p = jnp.pad(gla_w_a2[j], ((0, LANES - GLA_GATE_RANK), (0, 0)))
            o_p, s_p = _gla_core(qkvr[0], glr[0], w_a2p, gla_b_a[j], gla_norm_g[j], None, nb, t)
            chunk = lambda a: jnp.pad(sample_seq(a, SAMPLE_ROWS), ((0, 0), (0, GLA_CHUNK - SAMPLE_ROWS), (0, 0))
                                      ).reshape(sb * GLA_CHUNK, a.shape[-1])
            o_s, s_s = _gla_core(chunk(qkvr[1]), chunk(glr[1]), w_a2p, gla_b_a[j], gla_norm_g[j],
                                 state_gla[j], sb, st)
            o_s = o_s.reshape(sb, GLA_CHUNK, hv)[:, :SAMPLE_ROWS].reshape(sb * SAMPLE_ROWS, hv)
            gla_p.append(s_p)
            gla_s.append(s_s)
            z = _res_mm((o_p, o_s), gla_w_o, (j,), xf, 1.0)
        else:
            lam_init = 0.8 - 0.6 * math.exp(-0.3 * i)
            qkv = _proj(xb, diff_w_in, (j,), 0, 3 * d, 512)
            k_rows.append(tuple(a[:, d:2 * d] for a in qkv))
            v_rows.append(tuple(a[:, 2 * d:] for a in qkv))
            o_p = _flash_diff(qkv[0], rel_bias, diff_lambda[j], diff_norm_g[j], nb, lam_init)
            o_s = _paged_diff(sample_seq(qkv[1][:, :d], st), sample_seq(k_rows[-1][1], SAMPLE_KEY_ROWS),
                              sample_seq(v_rows[-1][1], SAMPLE_KEY_ROWS), cache_k, cache_v, page_table, j,
                              rel_bias, diff_lambda[j], diff_norm_g[j], lam_init)
            z = _res_mm((o_p, sample_rows(o_s)), diff_w_o, (j,), xf, 1.0)
        xf, xb = layer_norm(z, ln_g[i, 1], ln_b[i, 1])
        xf, xb = ffn(xf, xb, i, 1)

    y = [_layer_norm(a, ln_g[DEPTH - 1, 2], ln_b[DEPTH - 1, 2], final=True)[0] for a in xf]
    heads = lambda a, n_seq, rows: a.reshape(n_seq, rows, DIFF_HEADS, hw)
    return (y[0].reshape(nb, t, d),
            sample_seq(y[1], st),
            jnp.stack(gla_p),
            jnp.stack(gla_s),
            jnp.stack([heads(k[0], nb, t) for k in k_rows]),
            jnp.stack([heads(v[0], nb, t) for v in v_rows]),
            jnp.stack([heads(sample_seq(k[1], st), sb, st) for k in k_rows]),
            jnp.stack([heads(sample_seq(v[1], st), sb, st) for v in v_rows]))
```

```python
import functools
import math

import jax
import jax.numpy as jnp
import numpy as np
from jax import lax
from jax.experimental import pallas as pl
from jax.experimental.pallas import tpu as pltpu

F32 = jnp.float32
BF16 = jnp.bfloat16

DEPTH = 2
N_MIXERS = 2
GLA_HEADS = 4
GLA_GATE_RANK = 16
GLA_TAU = 16.0
GLA_CHUNK = 64
DIFF_HEADS = 16
PAGE_SIZE = 128
REL_BUCKETS = 32
REL_MAX_DIST = 128
DEEPNORM_ALPHA = (2.0 * DEPTH) ** 0.25
LN_EPS = 1e-5
NEG = -1e30
LOG2E = 1.0 / math.log(2.0)

LANES = 128
MIB = 1024 * 1024
SPLIT_K_ABOVE = 8192
FLASH_TILE = 512
PAGES_PER_STEP = 4
SAMPLE_ROWS = 8
SAMPLE_KEY_ROWS = 8


def _row_tile(m, cap):
    for t in range(min(m, cap) // 16 * 16, 15, -16):
        if m % t == 0:
            return t
    raise ValueError(f"no row tile for {m} rows")


def _params(sem, vmem_mib):
    return pltpu.CompilerParams(dimension_semantics=sem, vmem_limit_bytes=vmem_mib * MIB)


def _ln_apply(z, mean, rstd, g, b):
    return (z - mean) * rstd * g + b


def _ln_kernel(z_ref, g_ref, b_ref, *out_refs, final):
    z = z_ref[...]
    mean = jnp.mean(z, axis=-1, keepdims=True)
    zc = z - mean
    rstd = lax.rsqrt(jnp.mean(zc * zc, axis=-1, keepdims=True) + LN_EPS)
    y = _ln_apply(z, mean, rstd, g_ref[...], b_ref[...])
    if final:
        out_refs[0][...] = y
    else:
        xb_ref, mean_ref, rstd_ref = out_refs
        xb_ref[...] = y.astype(BF16)
        mean_ref[...] = jnp.broadcast_to(mean, mean_ref.shape)
        rstd_ref[...] = jnp.broadcast_to(rstd, rstd_ref.shape)


def _layer_norm(z, g, b, final=False):
    m, d = z.shape
    tr = _row_tile(m, 256)
    rows = lambda w: pl.BlockSpec((tr, w), lambda i: (i, 0))
    if final:
        out_specs, out_shape = [rows(d)], [jax.ShapeDtypeStruct((m, d), F32)]
    else:
        out_specs = [rows(d), rows(LANES), rows(LANES)]
        out_shape = [jax.ShapeDtypeStruct((m, d), BF16), jax.ShapeDtypeStruct((m, LANES), F32),
                     jax.ShapeDtypeStruct((m, LANES), F32)]
    return pl.pallas_call(
        functools.partial(_ln_kernel, final=final),
        grid=(m // tr,),
        in_specs=[rows(d), pl.BlockSpec((1, d), lambda i: (0, 0)), pl.BlockSpec((1, d), lambda i: (0, 0))],
        out_specs=out_specs,
        out_shape=out_shape,
        compiler_params=_params(("arbitrary",), 40),
        name="layer_norm",
    )(z, g.reshape(1, d), b.reshape(1, d))


def _cast_kernel(x_ref, o_ref):
    o_ref[...] = x_ref[...].astype(BF16)


def _to_bf16(x):
    m, d = x.shape
    tr = _row_tile(m, 256)
    return pl.pallas_call(
        _cast_kernel,
        grid=(m // tr,),
        in_specs=[pl.BlockSpec((tr, d), lambda i: (i, 0))],
        out_specs=pl.BlockSpec((tr, d), lambda i: (i, 0)),
        out_shape=jax.ShapeDtypeStruct((m, d), BF16),
        compiler_params=_params(("arbitrary",), 40),
        name="to_bf16",
    )(x)


def _w_spec(w, lead, k, tn, col0):
    none = (None,) * len(lead)
    return pl.BlockSpec(none + (k, tn), lambda n, m: tuple(lead) + (0, col0 + n))


def _row_specs(tm, ms, width, col):
    return [pl.BlockSpec((tm, width), lambda n, i: (i, col(n))),
            pl.BlockSpec((ms, width), lambda n, i: (0, col(n)))]


def _pair_shapes(m, ms, n_cols, dtype):
    return [jax.ShapeDtypeStruct((m, n_cols), dtype), jax.ShapeDtypeStruct((ms, n_cols), dtype)]


def _proj_kernel(x_ref, xs_ref, w_ref, o_ref, os_ref, wb_ref):
    @pl.when(pl.program_id(1) == 0)
    def _():
        wb_ref[...] = w_ref[...].astype(BF16)
        os_ref[...] = jnp.dot(xs_ref[...], wb_ref[...], preferred_element_type=F32).astype(os_ref.dtype)

    o_ref[...] = jnp.dot(x_ref[...], wb_ref[...], preferred_element_type=F32).astype(o_ref.dtype)


def _proj(x, w, lead, col0, n_cols, tn, out_dtype=F32):
    (m, k), ms = x[0].shape, x[1].shape[0]
    tm = _row_tile(m, 1024)
    return pl.pallas_call(
        _proj_kernel,
        grid=(n_cols // tn, m // tm),
        in_specs=_row_specs(tm, ms, k, lambda n: 0) + [_w_spec(w, lead, k, tn, col0)],
        out_specs=_row_specs(tm, ms, tn, lambda n: n),
        out_shape=_pair_shapes(m, ms, n_cols, out_dtype),
        scratch_shapes=[pltpu.VMEM((k, tn), BF16)],
        compiler_params=_params(("arbitrary", "arbitrary"), 56),
        name="proj",
    )(x[0], x[1], w)


def _ffn_up_kernel(x_ref, xs_ref, wg_ref, wu_ref, h_ref, hs_ref, wgb_ref, wub_ref):
    def swiglu(x):
        g = jnp.dot(x, wgb_ref[...], preferred_element_type=F32)
        u = jnp.dot(x, wub_ref[...], preferred_element_type=F32)
        return (g * jax.nn.sigmoid(g) * u).astype(BF16)

    @pl.when(pl.program_id(1) == 0)
    def _():
        wgb_ref[...] = wg_ref[...].astype(BF16)
        wub_ref[...] = wu_ref[...].astype(BF16)
        hs_ref[...] = swiglu(xs_ref[...])

    h_ref[...] = swiglu(x_ref[...])


def _ffn_up(x, w_gate, w_up, lead):
    (m, k), ms = x[0].shape, x[1].shape[0]
    f = w_gate.shape[-1]
    tm = _row_tile(m, 1024)
    tn = 256
    return pl.pallas_call(
        _ffn_up_kernel,
        grid=(f // tn, m // tm),
        in_specs=_row_specs(tm, ms, k, lambda n: 0)
                 + [_w_spec(w_gate, lead, k, tn, 0), _w_spec(w_up, lead, k, tn, 0)],
        out_specs=_row_specs(tm, ms, tn, lambda n: n),
        out_shape=_pair_shapes(m, ms, f, BF16),
        scratch_shapes=[pltpu.VMEM((k, tn), BF16), pltpu.VMEM((k, tn), BF16)],
        compiler_params=_params(("arbitrary", "arbitrary"), 56),
        name="ffn_up",
    )(x[0], x[1], w_gate, w_up)


def _res_mm_kernel(*refs, alpha, scale, has_partial, res_mode):
    h_ref, hs_ref, w_ref = refs[:3]
    o_ref, os_ref, wb_ref = refs[-3:]
    extra = list(refs[3:-3])
    partial = [extra.pop(0), extra.pop(0)] if has_partial else None

    def residual(group):
        if res_mode == "plain":
            return extra[group][...]
        z, mean, rstd = extra[group], extra[2 + group], extra[4 + group]
        wide = lambda a: pltpu.repeat(a[...], z.shape[1] // LANES, axis=1)
        return _ln_apply(z[...], wide(mean), wide(rstd), extra[6][...], extra[7][...])

    def epilogue(y, group):
        if has_partial:
            y = partial[group][...] + y
        if res_mode is not None:
            y = alpha * residual(group) + scale * y
        return y

    @pl.when(pl.program_id(1) == 0)
    def _():
        wb_ref[...] = w_ref[...].astype(BF16)
        os_ref[...] = epilogue(jnp.dot(hs_ref[...], wb_ref[...], preferred_element_type=F32), 1)

    o_ref[...] = epilogue(jnp.dot(h_ref[...], wb_ref[...], preferred_element_type=F32), 0)


def _res_mm_pass(h, w, lead, kb, n_kb, partial, res, scale, tm_cap):
    (m, k), ms = h[0].shape, h[1].shape[0]
    kh = k // n_kb
    n_cols = w.shape[-1]
    tn = 512
    tm = _row_tile(m, tm_cap)
    none = (None,) * len(lead)
    tiles = lambda: _row_specs(tm, ms, tn, lambda n: n)
    in_specs = _row_specs(tm, ms, kh, lambda n: kb) + [
        pl.BlockSpec(none + (kh, tn), lambda n, i: tuple(lead) + (kb, n))]
    args = [h[0], h[1], w]
    if partial is not None:
        in_specs += tiles()
        args += list(partial)
    if res is not None and res[0] == "plain":
        in_specs += tiles()
        args += list(res[1])
    elif res is not None:
        _, z, mean, rstd, g, b = res
        vec = pl.BlockSpec((1, tn), lambda n, i: (0, n))
        in_specs += tiles() + 2 * _row_specs(tm, ms, LANES, lambda n: 0) + [vec, vec]
        args += list(z) + list(mean) + list(rstd) + [g.reshape(1, n_cols), b.reshape(1, n_cols)]
    return pl.pallas_call(
        functools.partial(_res_mm_kernel, alpha=DEEPNORM_ALPHA, scale=scale,
                          has_partial=partial is not None, res_mode=None if res is None else res[0]),
        grid=(n_cols // tn, m // tm),
        in_specs=in_specs,
        out_specs=_row_specs(tm, ms, tn, lambda n: n),
        out_shape=_pair_shapes(m, ms, n_cols, F32),
        scratch_shapes=[pltpu.VMEM((kh, tn), BF16)],
        compiler_params=_params(("arbitrary", "arbitrary"), 56),
        name="res_mm",
    )(*args)


def _res_mm(h, w, lead, res, scale):
    k = h[0].shape[1]
    if k > SPLIT_K_ABOVE and (k // 2) % LANES == 0:
        part = _res_mm_pass(h, w, lead, 0, 2, None, None, scale, 512)
        return _res_mm_pass(h, w, lead, 1, 2, part, res, scale, 512)
    return _res_mm_pass(h, w, lead, 0, 1, None, res, scale, 1024)


def _log_sigmoid(x):
    return jnp.minimum(x, 0.0) - jnp.log(1.0 + jnp.exp(-jnp.abs(x)))


def _gla_kernel(*refs, chunk, valid, has_s0, q_scale, dk, dv):
    if has_s0:
        (q_ref, k_ref, v_ref, r_ref, glr_ref, wa2_ref, ba_ref, ng_ref, s0_ref,
         o_ref, sfin_ref, s_ref) = refs
    else:
        (q_ref, k_ref, v_ref, r_ref, glr_ref, wa2_ref, ba_ref, ng_ref,
         o_ref, sfin_ref, s_ref) = refs
    c = pl.program_id(1)

    @pl.when(c == 0)
    def _():
        if has_s0:
            s_ref[...] = s0_ref[...]
        else:
            s_ref[...] = jnp.zeros_like(s_ref)

    hk = q_ref.shape[-1]
    sub = GLA_CHUNK if chunk % GLA_CHUNK == 0 else chunk
    assert chunk in (sub, 2 * sub)
    two = chunk == 2 * sub
    pre = jnp.dot(glr_ref[...].astype(BF16), wa2_ref[...].astype(BF16),
                  preferred_element_type=F32) + ba_ref[...]
    log_a = _log_sigmoid(pre) / GLA_TAU
    k = k_ref[...]
    if valid < chunk:
        row_ok = lax.broadcasted_iota(jnp.int32, (chunk, 1), 0) < valid
        log_a = jnp.where(row_ok, log_a, 0.0)
        k = jnp.where(row_ok, k, 0.0)
    ri = lax.broadcasted_iota(jnp.int32, (chunk, chunk), 0)
    ci = lax.broadcasted_iota(jnp.int32, (chunk, chunk), 1)
    causal = (ri >= ci) & ((ri >= sub) == (ci >= sub))
    b = jnp.dot(jnp.where(causal, 1.0, 0.0), log_a, precision=lax.Precision.HIGHEST,
                preferred_element_type=F32)
    b_end = b[chunk - 1:chunk, :]
    q = q_ref[...] * q_scale
    q_t = (q * jnp.exp(b)).astype(BF16)
    k_t = (k * jnp.exp(-b)).astype(BF16)
    if two:
        first = lax.broadcasted_iota(jnp.int32, (chunk, 1), 0) < sub
        b_mid = b[sub - 1:sub, :]
        k_end1 = k * jnp.exp(jnp.where(first, b_mid, b_end) - b)
        q_s0 = (q * jnp.exp(jnp.where(first, b, b + b_mid))).astype(BF16)
        k_end = (k_end1 * jnp.where(first, jnp.exp(b_end), 1.0)).astype(BF16)
        k_end1 = k_end1.astype(BF16)
        b_tot = b_mid + b_end
        cross = (ri >= sub) & (ci < sub)
    else:
        q_s0 = q_t
        k_end = (k * jnp.exp(b_end - b)).astype(BF16)
        b_tot = b_end
    decay_col = jnp.exp(jnp.broadcast_to(b_tot, (LANES, hk)).T[:, :1])
    last = c == pl.num_programs(1) - 1
    nt = (((1,), (1,)), ((), ()))
    for h in range(GLA_HEADS):
        ks = slice(h * dk, (h + 1) * dk)
        vs = slice(h * dv, (h + 1) * dv)
        v = v_ref[:, vs].astype(BF16)
        att = jnp.where(causal, lax.dot_general(q_t[:, ks], k_t[:, ks], nt, preferred_element_type=F32), 0.0)
        if two:
            att = jnp.where(cross, lax.dot_general(q_t[:, ks], k_end1[:, ks], nt,
                                                   preferred_element_type=F32), att)
        s_old = s_ref[h]
        o = (jnp.dot(att.astype(BF16), v, preferred_element_type=F32)
             + jnp.dot(q_s0[:, ks], s_old.astype(BF16), preferred_element_type=F32))
        kv = lax.dot_general(k_end[:, ks], v, (((0,), (0,)), ((), ())), preferred_element_type=F32)
        s_new = decay_col[ks] * s_old + kv
        s_ref[h] = s_new

        @pl.when(last)
        def _(h=h, s_new=s_new):
            sfin_ref[h] = s_new

        oc = o - jnp.mean(o, axis=-1, keepdims=True)
        var = jnp.mean(oc * oc, axis=-1, keepdims=True)
        on = oc * lax.rsqrt(var + LN_EPS) * ng_ref[...]
        r = r_ref[:, vs]
        o_ref[:, vs] = (on * (r * jax.nn.sigmoid(r))).astype(BF16)


def _gla_core(qkvr, glr, w_a2p, b_a, norm_g, s0, n_seq, valid):
    m = qkvr.shape[0]
    hk = w_a2p.shape[-1]
    dk = hk // GLA_HEADS
    dv = norm_g.shape[-1]
    t = m // n_seq
    chunk = next((c for c in (2 * GLA_CHUNK, GLA_CHUNK) if t % c == 0), t)
    nc = t // chunk
    hv = GLA_HEADS * dv
    assert (2 * hk) % hv == 0
    vb0 = 2 * hk // hv
    has_s0 = s0 is not None
    row = lambda b, c: b * nc + c
    state = pl.BlockSpec((None, GLA_HEADS, dk, dv), lambda b, c: (b, 0, 0, 0))
    in_specs = [
        pl.BlockSpec((chunk, hk), lambda b, c: (row(b, c), 0)),
        pl.BlockSpec((chunk, hk), lambda b, c: (row(b, c), 1)),
        pl.BlockSpec((chunk, hv), lambda b, c: (row(b, c), vb0)),
        pl.BlockSpec((chunk, hv), lambda b, c: (row(b, c), vb0 + 1)),
        pl.BlockSpec((chunk, LANES), lambda b, c: (row(b, c), 0)),
        pl.BlockSpec((LANES, hk), lambda b, c: (0, 0)),
        pl.BlockSpec((1, hk), lambda b, c: (0, 0)),
        pl.BlockSpec((1, dv), lambda b, c: (0, 0)),
    ]
    args = [qkvr, qkvr, qkvr, qkvr, glr, w_a2p, b_a.reshape(1, hk), norm_g.reshape(1, dv)]
    if has_s0:
        in_specs.append(state)
        args.append(s0)
    return pl.pallas_call(
        functools.partial(_gla_kernel, chunk=chunk, valid=valid, has_s0=has_s0, q_scale=dk ** -0.5,
                          dk=dk, dv=dv),
        grid=(n_seq, nc),
        in_specs=in_specs,
        out_specs=[pl.BlockSpec((chunk, hv), lambda b, c: (row(b, c), 0)), state],
        out_shape=[jax.ShapeDtypeStruct((m, hv), BF16),
                   jax.ShapeDtypeStruct((n_seq, GLA_HEADS, dk, dv), F32)],
        scratch_shapes=[pltpu.VMEM((GLA_HEADS, dk, dv), F32)],
        compiler_params=_params(("arbitrary", "arbitrary"), 56),
        name="gla_core",
    )(*args)


def _t5_bucket(dist):
    n = jnp.maximum(dist, 0)
    max_exact = REL_BUCKETS // 2
    large = max_exact + (jnp.log(jnp.maximum(n, 1).astype(F32) / max_exact)
                         / math.log(REL_MAX_DIST / max_exact) * (REL_BUCKETS - max_exact)).astype(jnp.int32)
    large = jnp.minimum(large, REL_BUCKETS - 1)
    return jnp.where(n < max_exact, n, large)


def _bias_kernel(qpos_ref, kpos_ref, qgrp_ref, kgrp_ref, rb_ref, o_ref):
    bucket = _t5_bucket(qpos_ref[...] - kpos_ref[...])
    rb = rb_ref[...]
    acc = jnp.zeros(o_ref.shape, F32)
    for i in range(REL_BUCKETS):
        acc = jnp.where(bucket == i, rb[:, i:i + 1], acc)
    o_ref[...] = jnp.where(qgrp_ref[...] == kgrp_ref[...], acc, NEG)


def _bias_table(qpos, kpos, rb_rows, tr, tc, qgrp, kgrp):
    r, c = qpos.shape[0], kpos.shape[0]
    col = lambda a: a.reshape(r, 1).astype(jnp.int32)
    row = lambda a: a.reshape(1, c).astype(jnp.int32)
    return pl.pallas_call(
        _bias_kernel,
        grid=(r // tr, c // tc),
        in_specs=[pl.BlockSpec((tr, 1), lambda i, j: (i, 0)),
                  pl.BlockSpec((1, tc), lambda i, j: (0, j)),
                  pl.BlockSpec((tr, 1), lambda i, j: (i, 0)),
                  pl.BlockSpec((1, tc), lambda i, j: (0, j)),
                  pl.BlockSpec((tr, REL_BUCKETS), lambda i, j: (i, 0))],
        out_specs=pl.BlockSpec((tr, tc), lambda i, j: (i, j)),
        out_shape=jax.ShapeDtypeStruct((r, c), F32),
        compiler_params=_params(("arbitrary", "arbitrary"), 32),
        name="t5_bias",
    )(col(qpos), row(kpos), col(qgrp), row(kgrp), rb_rows)


def _toeplitz_bias_kernel(rb_ref, o_ref, *, t, mult):
    off = pl.program_id(1)
    w = 2 * t
    u = lax.broadcasted_iota(jnp.int32, (1, w), 1)
    dist = off * t + jnp.where(u < t, -u, w - u)
    bucket = _t5_bucket(dist)
    rb = rb_ref[...]
    g = jnp.zeros((1, w), F32)
    for i in range(REL_BUCKETS):
        g = jnp.where(bucket == i, rb[:, i:i + 1], g)
    x = jnp.broadcast_to(g * mult, (t, w))
    o_ref[...] = pltpu.roll(x, 0, 1, stride=1, stride_axis=0)[:, :t]


def _toeplitz_bias(rel_bias, n_off, t, mult):
    n_heads = rel_bias.shape[1]
    return pl.pallas_call(
        functools.partial(_toeplitz_bias_kernel, t=t, mult=mult),
        grid=(n_heads, n_off),
        in_specs=[pl.BlockSpec((None, 1, REL_BUCKETS), lambda h, o: (h, 0, 0))],
        out_specs=pl.BlockSpec((None, None, t, t), lambda h, o: (h, o, 0, 0)),
        out_shape=jax.ShapeDtypeStruct((n_heads, n_off, t, t), F32),
        compiler_params=_params(("arbitrary", "arbitrary"), 32),
        name="t5_bias_tiles",
    )(rel_bias.T.reshape(n_heads, 1, REL_BUCKETS))


def _first_const_dist(limit):
    n = np.arange(1, limit + 1)
    large = 16 + (np.log(n.astype(np.float32) / np.float32(16)) / np.float32(math.log(8.0))
                  * np.float32(16)).astype(np.int32)
    bucket = np.where(n < 16, n, np.minimum(large, REL_BUCKETS - 1))
    not_last = np.nonzero(bucket != REL_BUCKETS - 1)[0]
    return int(n[not_last[-1]] + 1) if not_last.size else 1


def _lambda(lam_ref, lam_init):
    lp = lam_ref[...]
    a = jnp.sum(lp[0:1] * lp[1:2], axis=-1, keepdims=True)
    b = jnp.sum(lp[2:3] * lp[3:4], axis=-1, keepdims=True)
    return jnp.exp(a) - jnp.exp(b) + lam_init


def _flash_kernel(q_ref, k_ref, v_ref, bias_ref, lam_ref, ng_ref, o_ref,
                  kb_ref, vb_ref, m_ref, l_ref, acc_ref, *, t, hd, n_near, lam_init, scale):
    qi = pl.program_id(2)

    @pl.when(qi == 0)
    def _():
        kb_ref[...] = k_ref[...].astype(BF16)
        vb_ref[...] = v_ref[...].astype(BF16)

    q = q_ref[...] * (scale * LOG2E)
    lane = lax.broadcasted_iota(jnp.int32, q.shape, 1)
    q2 = jnp.concatenate([jnp.where(lane < hd, q, 0.0), jnp.where(lane >= hd, q, 0.0)],
                         axis=0).astype(BF16)
    m_ref[...] = jnp.full(m_ref.shape, NEG, F32)
    l_ref[...] = jnp.zeros(l_ref.shape, F32)
    acc_ref[...] = jnp.zeros(acc_ref.shape, F32)
    far_bias = bias_ref[n_near - 1, t - 1:t, 0:1]

    def tile(kj, bias, mask):
        rows = pl.ds(pl.multiple_of(kj * t, t), t)
        kt = kb_ref[rows, :]
        vt = vb_ref[rows, :]
        for c in range(2):
            rc = slice(c * t, (c + 1) * t)
            s = lax.dot_general(q2[rc], kt, (((1,), (1,)), ((), ())),
                                preferred_element_type=F32)
            s = s + bias
            if mask is not None:
                s = jnp.where(mask, s, NEG)
            m_old = m_ref[rc, :]
            m_new = jnp.maximum(m_old, jnp.max(s, axis=-1, keepdims=True))
            alpha = jnp.exp2(m_old - m_new)
            p = jnp.exp2(s - pltpu.repeat(m_new, t // LANES, axis=1))
            l_ref[rc, :] = alpha * l_ref[rc, :] + jnp.sum(p, axis=-1, keepdims=True)
            acc_ref[rc, :] = (pltpu.repeat(alpha, 2 * hd // LANES, axis=1) * acc_ref[rc, :]
                              + jnp.dot(p.astype(BF16), vt, preferred_element_type=F32))
            m_ref[rc, :] = m_new

    def far_body(kj, carry):
        tile(kj, far_bias, None)
        return carry

    lax.fori_loop(0, jnp.maximum(qi - (n_near - 1), 0), far_body, 0)
    for off in range(n_near - 1, 0, -1):
        @pl.when(qi >= off)
        def _(off=off):
            tile(qi - off, bias_ref[off], None)
    ri = lax.broadcasted_iota(jnp.int32, (t, t), 0)
    ci = lax.broadcasted_iota(jnp.int32, (t, t), 1)
    tile(qi, bias_ref[0], ri >= ci)

    lam = _lambda(lam_ref, lam_init)
    a = acc_ref[...] / pltpu.repeat(l_ref[...], 2 * hd // LANES, axis=1)
    o = a[:t] - lam * a[t:]
    y = o * lax.rsqrt(jnp.mean(o * o, axis=-1, keepdims=True) + LN_EPS) * ng_ref[...]
    o_ref[...] = (y * (1.0 - lam_init)).astype(BF16)


def _flash_diff(q, k, v, rel_bias, lam_params, norm_g, n_seq, lam_init):
    m, d = q.shape
    t_seq = m // n_seq
    hw = d // DIFF_HEADS
    hd = hw // 2
    t = next(c for c in range(FLASH_TILE, 0, -LANES) if t_seq % c == 0)
    nq = t_seq // t
    n_near = min(nq, -(-(_first_const_dist(t_seq) - 1) // t) + 1)
    bias = _toeplitz_bias(rel_bias, n_near, t, LOG2E)
    return pl.pallas_call(
        functools.partial(_flash_kernel, t=t, hd=hd, n_near=n_near, lam_init=lam_init, scale=hd ** -0.5),
        grid=(n_seq, DIFF_HEADS, nq),
        in_specs=[pl.BlockSpec((t, hw), lambda b, h, i: (b * nq + i, h)),
                  pl.BlockSpec((t_seq, hw), lambda b, h, i: (b, h)),
                  pl.BlockSpec((t_seq, hw), lambda b, h, i: (b, h)),
                  pl.BlockSpec((None, n_near, t, t), lambda b, h, i: (h, 0, 0, 0)),
                  pl.BlockSpec((4, hd), lambda b, h, i: (0, 0)),
                  pl.BlockSpec((1, hw), lambda b, h, i: (0, 0))],
        out_specs=pl.BlockSpec((t, hw), lambda b, h, i: (b * nq + i, h)),
        out_shape=jax.ShapeDtypeStruct((m, d), BF16),
        scratch_shapes=[pltpu.VMEM((t_seq, hw), BF16), pltpu.VMEM((t_seq, hw), BF16),
                        pltpu.VMEM((2 * t, LANES), F32), pltpu.VMEM((2 * t, LANES), F32),
                        pltpu.VMEM((2 * t, hw), F32)],
        compiler_params=_params(("arbitrary", "arbitrary", "arbitrary"), 48),
        name="flash_diff",
    )(q, k, v, bias, lam_params, norm_g.reshape(1, hw))


def _paged_kernel(pt_ref, q_ref, kn_ref, vn_ref, tab_ref, tabn_ref, lam_ref, ng_ref, *rest,
                  pages, q_tok, lam_init, scale):
    k_refs = rest[:pages]
    v_refs = rest[pages:2 * pages]
    o_ref, m_ref, l_ref, acc_ref = rest[2 * pages:]
    s_id = pl.program_id(1)
    rows, hw = q_ref.shape
    half = rows // 2
    q = q_ref[...]

    def scores(kmat):
        return lax.dot_general(q, kmat, (((1,), (1,)), ((), ())), preferred_element_type=F32) * scale

    def update(s, vmat):
        m_old = m_ref[...]
        m_new = jnp.maximum(m_old, jnp.max(s, axis=-1, keepdims=True))
        alpha = jnp.exp(m_old - m_new)
        p = jnp.exp(s - pltpu.repeat(m_new, s.shape[1] // LANES, axis=1))
        l_ref[...] = alpha * l_ref[...] + jnp.sum(p, axis=-1, keepdims=True)
        acc_ref[...] = (pltpu.repeat(alpha, hw // LANES, axis=1) * acc_ref[...]
                        + jnp.dot(p.astype(BF16), vmat, preferred_element_type=F32))
        m_ref[...] = m_new

    @pl.when(s_id == 0)
    def _():
        m_ref[...] = jnp.full(m_ref.shape, NEG, F32)
        l_ref[...] = jnp.zeros(l_ref.shape, F32)
        acc_ref[...] = jnp.zeros(acc_ref.shape, F32)
        s = scores(kn_ref[...].astype(BF16)) + tabn_ref[...]
        tq = lax.broadcasted_iota(jnp.int32, s.shape, 0) % q_tok
        tk = lax.broadcasted_iota(jnp.int32, s.shape, 1) // DIFF_HEADS
        update(jnp.where(tk <= tq, s, NEG), vn_ref[...].astype(BF16))

    kp = jnp.concatenate([r[...].astype(BF16) for r in k_refs], axis=0)
    vp = jnp.concatenate([r[...].astype(BF16) for r in v_refs], axis=0)
    update(scores(kp) + tab_ref[...], vp)

    @pl.when(s_id == pl.num_programs(1) - 1)
    def _():
        a = acc_ref[...] / pltpu.repeat(l_ref[...], hw // LANES, axis=1)
        lam = _lambda(lam_ref, lam_init)
        o = a[:half] - lam * a[half:]
        y = o * lax.rsqrt(jnp.mean(o * o, axis=-1, keepdims=True) + LN_EPS) * ng_ref[...]
        o_ref[...] = (y * (1.0 - lam_init)).astype(o_ref.dtype)


def _paged_diff(q, k_new, v_new, cache_k, cache_v, page_table, layer, rel_bias, lam_params, norm_g,
                lam_init):
    nb, tq, d = q.shape
    tk = k_new.shape[1]
    assert tq <= tk
    hw = d // DIFF_HEADS
    hd = hw // 2
    n_pages = page_table.shape[1]
    past = n_pages * PAGE_SIZE
    pages = next(p for p in (PAGES_PER_STEP, 2, 1) if n_pages % p == 0)
    n_steps = n_pages // pages
    rows = 2 * DIFF_HEADS * tq
    page_rows = PAGE_SIZE * DIFF_HEADS
    assert pages * PAGE_SIZE + 1 >= _first_const_dist(past + tq)
    q5 = q.reshape(nb, tq, DIFF_HEADS, 2, hd).transpose(0, 3, 2, 1, 4)
    same_c = jnp.arange(2)[:, None] == jnp.arange(2)[None, :]
    qd = jnp.where(same_c[None, :, None, None, :, None], q5[:, :, :, :, None, :], 0.0)
    qd = qd.reshape(nb, rows, hw).astype(BF16)
    k_rows = k_new.reshape(nb, tk * DIFF_HEADS, hw)
    v_rows = v_new.reshape(nb, tk * DIFF_HEADS, hw)
    qpos = jnp.tile(past + jnp.arange(tq), 2 * DIFF_HEADS)
    qgrp = jnp.tile(jnp.repeat(jnp.arange(DIFF_HEADS), tq), 2)
    rb_rows = jnp.tile(jnp.repeat(rel_bias.T, tq, axis=0), (2, 1))
    col_tok = jnp.repeat(jnp.arange(pages * PAGE_SIZE), DIFF_HEADS)
    col_grp = jnp.tile(jnp.arange(DIFF_HEADS), pages * PAGE_SIZE)
    kpos = jnp.concatenate([jnp.zeros_like(col_tok), (n_steps - 1) * pages * PAGE_SIZE + col_tok])
    tab = _bias_table(qpos, kpos, rb_rows, rows, page_rows, qgrp, jnp.tile(col_grp, 2))
    tab = tab.reshape(rows, 2, pages * page_rows).transpose(1, 0, 2)
    tab_new = _bias_table(qpos, past + jnp.repeat(jnp.arange(tk), DIFF_HEADS), rb_rows, rows,
                          tk * DIFF_HEADS, qgrp, jnp.tile(jnp.arange(DIFF_HEADS), tk))
    ck = cache_k.reshape(cache_k.shape[0], cache_k.shape[1], page_rows, hw)
    cv = cache_v.reshape(cache_v.shape[0], cache_v.shape[1], page_rows, hw)

    def page_spec(i):
        return pl.BlockSpec((None, None, page_rows, hw),
                            lambda b, s, pt: (layer, pt[b, s * pages + i], 0, 0))

    grid_spec = pltpu.PrefetchScalarGridSpec(
        num_scalar_prefetch=1,
        grid=(nb, n_steps),
        in_specs=[pl.BlockSpec((None, rows, hw), lambda b, s, pt: (b, 0, 0)),
                  pl.BlockSpec((None, tk * DIFF_HEADS, hw), lambda b, s, pt: (b, 0, 0)),
                  pl.BlockSpec((None, tk * DIFF_HEADS, hw), lambda b, s, pt: (b, 0, 0)),
                  pl.BlockSpec((None, rows, pages * page_rows),
                               lambda b, s, pt: (jnp.where(s == n_steps - 1, 1, 0), 0, 0)),
                  pl.BlockSpec((rows, tk * DIFF_HEADS), lambda b, s, pt: (0, 0)),
                  pl.BlockSpec((4, hd), lambda b, s, pt: (0, 0)),
                  pl.BlockSpec((1, hw), lambda b, s, pt: (0, 0))]
                 + [page_spec(i) for i in range(pages)] + [page_spec(i) for i in range(pages)],
        out_specs=pl.BlockSpec((None, rows // 2, hw), lambda b, s, pt: (b, 0, 0)),
        scratch_shapes=[pltpu.VMEM((rows, LANES), F32), pltpu.VMEM((rows, LANES), F32),
                        pltpu.VMEM((rows, hw), F32)],
    )
    out = pl.pallas_call(
        functools.partial(_paged_kernel, pages=pages, q_tok=tq, lam_init=lam_init, scale=hd ** -0.5),
        grid_spec=grid_spec,
        out_shape=jax.ShapeDtypeStruct((nb, rows // 2, hw), BF16),
        compiler_params=_params(("arbitrary", "arbitrary"), 56),
        name="paged_diff",
    )(page_table, qd, k_rows, v_rows, tab, tab_new, lam_params, norm_g.reshape(1, hw),
      *([ck] * pages), *([cv] * pages))
    return out.reshape(nb, DIFF_HEADS, tq, hw).transpose(0, 2, 1, 3).reshape(nb, tq, d)


def kernel(x_prompt, x_sample, state_gla, cache_k, cache_v, page_table, ln_g, ln_b, ffn_w_gate, ffn_w_up,
           ffn_w_down, gla_w_in, gla_w_a2, gla_b_a, gla_norm_g, gla_w_o, diff_w_in, diff_lambda, diff_norm_g,
           diff_w_o, rel_bias):
    nb, t, d = x_prompt.shape
    sb, st, _ = x_sample.shape
    hw = d // DIFF_HEADS
    assert st <= SAMPLE_KEY_ROWS <= SAMPLE_ROWS
    xs = jnp.pad(x_sample, ((0, 0), (0, SAMPLE_ROWS - st), (0, 0))).reshape(sb * SAMPLE_ROWS, d)
    x0 = (x_prompt.reshape(nb * t, d), xs)
    xf = ("plain", x0)
    xb = tuple(_to_bf16(a) for a in x0)
    gla_p, gla_s, k_rows, v_rows = [], [], [], []

    def layer_norm(z, g, b):
        outs = [_layer_norm(a, g, b) for a in z]
        xb, mean, rstd = (tuple(o[i] for o in outs) for i in range(3))
        return ("ln", z, mean, rstd, g, b), xb

    def ffn(xf, xb, i, s):
        h = _ffn_up(xb, ffn_w_gate, ffn_w_up, (i, s))
        z = _res_mm(h, ffn_w_down, (i, s), xf, 0.5)
        if (i, s) == (DEPTH - 1, 1):
            return z, None
        return layer_norm(z, ln_g[i, 2 * s], ln_b[i, 2 * s])

    def sample_seq(a, rows):
        return a.reshape(sb, SAMPLE_ROWS, a.shape[-1])[:, :rows]

    def sample_rows(a):
        return jnp.pad(a, ((0, 0), (0, SAMPLE_ROWS - a.shape[1]), (0, 0))).reshape(sb * SAMPLE_ROWS, a.shape[-1])

    for i in range(DEPTH):
        xf, xb = ffn(xf, xb, i, 0)
        j = i // N_MIXERS
        if i % N_MIXERS == 0:
            hk = gla_w_a2.shape[-1]
            hv = gla_w_o.shape[1]
            qkvr = _proj(xb, gla_w_in, (j,), 0, 2 * hk + 2 * hv, 512)
            w_g = jnp.pad(gla_w_in[j][:, 2 * hk + 2 * hv:], ((0, 0), (0, LANES - GLA_GATE_RANK)))
            glr = _proj(xb, w_g[None], (0,), 0, LANES, LANES)
            w_a2p = jnp.pad(gla_w_a2[j], ((0, LANES - GLA_GATE_RANK), (0, 0)))
            o_p, s_p = _gla_core(qkvr[0], glr[0], w_a2p, gla_b_a[j], gla_norm_g[j], None, nb, t)
            chunk = lambda a: jnp.pad(sample_seq(a, SAMPLE_ROWS), ((0, 0), (0, GLA_CHUNK - SAMPLE_ROWS), (0, 0))
                                      ).reshape(sb * GLA_CHUNK, a.shape[-1])
            o_s, s_s = _gla_core(chunk(qkvr[1]), chunk(glr[1]), w_a2p, gla_b_a[j], gla_norm_g[j],
                                 state_gla[j], sb, st)
            o_s = o_s.reshape(sb, GLA_CHUNK, hv)[:, :SAMPLE_ROWS].reshape(sb * SAMPLE_ROWS, hv)
            gla_p.append(s_p)
            gla_s.append(s_s)
            z = _res_mm((o_p, o_s), gla_w_o, (j,), xf, 1.0)
        else:
            lam_init = 0.8 - 0.6 * math.exp(-0.3 * i)
            q = _proj(xb, diff_w_in, (j,), 0, d, 512)
            k = _proj(xb, diff_w_in, (j,), d // 512, d, 512)
            v = _proj(xb, diff_w_in, (j,), 2 * d // 512, d, 512)
            k_rows.append(k)
            v_rows.append(v)
            o_p = _flash_diff(q[0], k[0], v[0], rel_bias, diff_lambda[j], diff_norm_g[j], nb, lam_init)
            o_s = _paged_diff(sample_seq(q[1], st), sample_seq(k[1], SAMPLE_KEY_ROWS),
                              sample_seq(v[1], SAMPLE_KEY_ROWS), cache_k, cache_v, page_table, j, rel_bias,
                              diff_lambda[j], diff_norm_g[j], lam_init)
            z = _res_mm((o_p, sample_rows(o_s)), diff_w_o, (j,), xf, 1.0)
        xf, xb = layer_norm(z, ln_g[i, 1], ln_b[i, 1])
        xf, xb = ffn(xf, xb, i, 1)

    y = [_layer_norm(a, ln_g[DEPTH - 1, 2], ln_b[DEPTH - 1, 2], final=True)[0] for a in xf]
    heads = lambda a, n_seq, rows: a.reshape(n_seq, rows, DIFF_HEADS, hw)
    return (y[0].reshape(nb, t, d),
            sample_seq(y[1], st),
            jnp.stack(gla_p),
            jnp.stack(gla_s),
            jnp.stack([heads(k[0], nb, t) for k in k_rows]),
            jnp.stack([heads(v[0], nb, t) for v in v_rows]),
            jnp.stack([heads(sample_seq(k[1], st), sb, st) for k in k_rows]),
            jnp.stack([heads(sample_seq(v[1], st), sb, st) for v in v_rows]))
```

```python
import functools
import math

import jax
import jax.numpy as jnp
import numpy as np
from jax import lax
from jax.experimental import pallas as pl
from jax.experimental.pallas import tpu as pltpu

F32 = jnp.float32
BF16 = jnp.bfloat16

DEPTH = 2
N_MIXERS = 2
GLA_HEADS = 4
GLA_GATE_RANK = 16
GLA_TAU = 16.0
GLA_CHUNK = 64
DIFF_HEADS = 16
PAGE_SIZE = 128
REL_BUCKETS = 32
REL_MAX_DIST = 128
DEEPNORM_ALPHA = (2.0 * DEPTH) ** 0.25
LN_EPS = 1e-5
NEG = -1e30
LOG2E = 1.0 / math.log(2.0)

LANES = 128
MXU_DEPTH = 256
MIB = 1024 * 1024
SPLIT_K_ABOVE = 8192
FLASH_TILE = 512
PAGES_PER_STEP = 4
SAMPLE_ROWS = 8
SAMPLE_KEY_ROWS = 8


def _row_tile(m, cap):
    for t in range(min(m, cap) // 16 * 16, 15, -16):
        if m % t == 0:
            return t
    raise ValueError(f"no row tile for {m} rows")


def _params(sem, vmem_mib):
    return pltpu.CompilerParams(dimension_semantics=sem, vmem_limit_bytes=vmem_mib * MIB)


def _ln_apply(z, mean, rstd, g, b):
    return (z - mean) * rstd * g + b


def _ln_kernel(z_ref, g_ref, b_ref, *out_refs, final):
    z = z_ref[...]
    mean = jnp.mean(z, axis=-1, keepdims=True)
    zc = z - mean
    rstd = lax.rsqrt(jnp.mean(zc * zc, axis=-1, keepdims=True) + LN_EPS)
    y = _ln_apply(z, mean, rstd, g_ref[...], b_ref[...])
    if final:
        out_refs[0][...] = y
    else:
        xb_ref, mean_ref, rstd_ref = out_refs
        xb_ref[...] = y.astype(BF16)
        mean_ref[...] = jnp.broadcast_to(mean, mean_ref.shape)
        rstd_ref[...] = jnp.broadcast_to(rstd, rstd_ref.shape)


def _layer_norm(z, g, b, final=False):
    m, d = z.shape
    tr = _row_tile(m, 256)
    rows = lambda w: pl.BlockSpec((tr, w), lambda i: (i, 0))
    if final:
        out_specs, out_shape = [rows(d)], [jax.ShapeDtypeStruct((m, d), F32)]
    else:
        out_specs = [rows(d), rows(LANES), rows(LANES)]
        out_shape = [jax.ShapeDtypeStruct((m, d), BF16), jax.ShapeDtypeStruct((m, LANES), F32),
                     jax.ShapeDtypeStruct((m, LANES), F32)]
    return pl.pallas_call(
        functools.partial(_ln_kernel, final=final),
        grid=(m // tr,),
        in_specs=[rows(d), pl.BlockSpec((1, d), lambda i: (0, 0)), pl.BlockSpec((1, d), lambda i: (0, 0))],
        out_specs=out_specs,
        out_shape=out_shape,
        compiler_params=_params(("arbitrary",), 40),
        name="layer_norm",
    )(z, g.reshape(1, d), b.reshape(1, d))


def _cast_kernel(x_ref, o_ref):
    o_ref[...] = x_ref[...].astype(BF16)


def _to_bf16(x):
    m, d = x.shape
    tr = _row_tile(m, 256)
    return pl.pallas_call(
        _cast_kernel,
        grid=(m // tr,),
        in_specs=[pl.BlockSpec((tr, d), lambda i: (i, 0))],
        out_specs=pl.BlockSpec((tr, d), lambda i: (i, 0)),
        out_shape=jax.ShapeDtypeStruct((m, d), BF16),
        compiler_params=_params(("arbitrary",), 40),
        name="to_bf16",
    )(x)


def _w_spec(w, lead, k, tn, col0):
    none = (None,) * len(lead)
    return pl.BlockSpec(none + (k, tn), lambda n, m: tuple(lead) + (0, col0 + n))


def _row_specs(tm, ms, width, col):
    return [pl.BlockSpec((tm, width), lambda n, i: (i, col(n))),
            pl.BlockSpec((ms, width), lambda n, i: (0, col(n)))]


def _pair_shapes(m, ms, n_cols, dtype):
    return [jax.ShapeDtypeStruct((m, n_cols), dtype), jax.ShapeDtypeStruct((ms, n_cols), dtype)]


def _proj_kernel(x_ref, xs_ref, w_ref, o_ref, os_ref, wb_ref):
    @pl.when(pl.program_id(1) == 0)
    def _():
        wb_ref[...] = w_ref[...].astype(BF16)
        os_ref[...] = jnp.dot(xs_ref[...], wb_ref[...], preferred_element_type=F32).astype(os_ref.dtype)

    o_ref[...] = jnp.dot(x_ref[...], wb_ref[...], preferred_element_type=F32).astype(o_ref.dtype)


def _proj(x, w, lead, col0, n_cols, tn, out_dtype=F32):
    (m, k), ms = x[0].shape, x[1].shape[0]
    tm = _row_tile(m, 1024)
    return pl.pallas_call(
        _proj_kernel,
        grid=(n_cols // tn, m // tm),
        in_specs=_row_specs(tm, ms, k, lambda n: 0) + [_w_spec(w, lead, k, tn, col0)],
        out_specs=_row_specs(tm, ms, tn, lambda n: n),
        out_shape=_pair_shapes(m, ms, n_cols, out_dtype),
        scratch_shapes=[pltpu.VMEM((k, tn), BF16)],
        compiler_params=_params(("arbitrary", "arbitrary"), 56),
        name="proj",
    )(x[0], x[1], w)


def _ffn_up_kernel(x_ref, xs_ref, wg_ref, wu_ref, h_ref, hs_ref, wgb_ref, wub_ref):
    def swiglu(x):
        g = jnp.dot(x, wgb_ref[...], preferred_element_type=F32)
        u = jnp.dot(x, wub_ref[...], preferred_element_type=F32)
        return (g * jax.nn.sigmoid(g) * u).astype(BF16)

    @pl.when(pl.program_id(1) == 0)
    def _():
        wgb_ref[...] = wg_ref[...].astype(BF16)
        wub_ref[...] = wu_ref[...].astype(BF16)
        hs_ref[...] = swiglu(xs_ref[...])

    h_ref[...] = swiglu(x_ref[...])


def _ffn_up(x, w_gate, w_up, lead):
    (m, k), ms = x[0].shape, x[1].shape[0]
    f = w_gate.shape[-1]
    tm = _row_tile(m, 1024)
    tn = 256
    return pl.pallas_call(
        _ffn_up_kernel,
        grid=(f // tn, m // tm),
        in_specs=_row_specs(tm, ms, k, lambda n: 0)
                 + [_w_spec(w_gate, lead, k, tn, 0), _w_spec(w_up, lead, k, tn, 0)],
        out_specs=_row_specs(tm, ms, tn, lambda n: n),
        out_shape=_pair_shapes(m, ms, f, BF16),
        scratch_shapes=[pltpu.VMEM((k, tn), BF16), pltpu.VMEM((k, tn), BF16)],
        compiler_params=_params(("arbitrary", "arbitrary"), 56),
        name="ffn_up",
    )(x[0], x[1], w_gate, w_up)


def _res_mm_kernel(*refs, alpha, scale, has_partial, res_mode):
    h_ref, hs_ref, w_ref = refs[:3]
    o_ref, os_ref, wb_ref = refs[-3:]
    extra = list(refs[3:-3])
    partial = [extra.pop(0), extra.pop(0)] if has_partial else None

    def residual(group):
        if res_mode == "plain":
            return extra[group][...]
        z, mean, rstd = extra[group], extra[2 + group], extra[4 + group]
        wide = lambda a: pltpu.repeat(a[...], z.shape[1] // LANES, axis=1)
        return _ln_apply(z[...], wide(mean), wide(rstd), extra[6][...], extra[7][...])

    def epilogue(y, group):
        if has_partial:
            y = partial[group][...] + y
        if res_mode is not None:
            y = alpha * residual(group) + scale * y
        return y

    @pl.when(pl.program_id(1) == 0)
    def _():
        wb_ref[...] = w_ref[...].astype(BF16)
        os_ref[...] = epilogue(jnp.dot(hs_ref[...], wb_ref[...], preferred_element_type=F32), 1)

    o_ref[...] = epilogue(jnp.dot(h_ref[...], wb_ref[...], preferred_element_type=F32), 0)


def _res_mm_pass(h, w, lead, k0, kh, partial, res, scale, tm_cap):
    m, ms = h[0].shape[0], h[1].shape[0]
    n_cols = w.shape[-1]
    tn = 512
    tm = _row_tile(m, tm_cap)
    none = (None,) * len(lead)
    tiles = lambda: _row_specs(tm, ms, tn, lambda n: n)
    el = pl.Element
    in_specs = [pl.BlockSpec((el(tm), el(kh)), lambda n, i: (i * tm, k0)),
                pl.BlockSpec((el(ms), el(kh)), lambda n, i: (0, k0)),
                pl.BlockSpec(none + (el(kh), el(tn)), lambda n, i: tuple(lead) + (k0, n * tn))]
    args = [h[0], h[1], w]
    if partial is not None:
        in_specs += tiles()
        args += list(partial)
    if res is not None and res[0] == "plain":
        in_specs += tiles()
        args += list(res[1])
    elif res is not None:
        _, z, mean, rstd, g, b = res
        vec = pl.BlockSpec((1, tn), lambda n, i: (0, n))
        in_specs += tiles() + 2 * _row_specs(tm, ms, LANES, lambda n: 0) + [vec, vec]
        args += list(z) + list(mean) + list(rstd) + [g.reshape(1, n_cols), b.reshape(1, n_cols)]
    return pl.pallas_call(
        functools.partial(_res_mm_kernel, alpha=DEEPNORM_ALPHA, scale=scale,
                          has_partial=partial is not None, res_mode=None if res is None else res[0]),
        grid=(n_cols // tn, m // tm),
        in_specs=in_specs,
        out_specs=_row_specs(tm, ms, tn, lambda n: n),
        out_shape=_pair_shapes(m, ms, n_cols, F32),
        scratch_shapes=[pltpu.VMEM((kh, tn), BF16)],
        compiler_params=_params(("arbitrary", "arbitrary"), 56),
        name="res_mm",
    )(*args)


def _res_mm(h, w, lead, res, scale):
    k = h[0].shape[1]
    k1 = -(-(k // 2) // MXU_DEPTH) * MXU_DEPTH
    if k > SPLIT_K_ABOVE and (k - k1) % LANES == 0:
        part = _res_mm_pass(h, w, lead, 0, k1, None, None, scale, 512)
        return _res_mm_pass(h, w, lead, k1, k - k1, part, res, scale, 512)
    return _res_mm_pass(h, w, lead, 0, k, None, res, scale, 1024)


def _log_sigmoid(x):
    return jnp.minimum(x, 0.0) - jnp.log(1.0 + jnp.exp(-jnp.abs(x)))


def _gla_kernel(*refs, chunk, valid, has_s0, q_scale, dk, dv):
    if has_s0:
        (q_ref, k_ref, v_ref, r_ref, glr_ref, wa2_ref, ba_ref, ng_ref, s0_ref,
         o_ref, sfin_ref, s_ref) = refs
    else:
        (q_ref, k_ref, v_ref, r_ref, glr_ref, wa2_ref, ba_ref, ng_ref,
         o_ref, sfin_ref, s_ref) = refs
    c = pl.program_id(1)

    @pl.when(c == 0)
    def _():
        if has_s0:
            s_ref[...] = s0_ref[...]
        else:
            s_ref[...] = jnp.zeros_like(s_ref)

    hk = q_ref.shape[-1]
    sub = GLA_CHUNK if chunk % GLA_CHUNK == 0 else chunk
    assert chunk in (sub, 2 * sub)
    two = chunk == 2 * sub
    pre = jnp.dot(glr_ref[...].astype(BF16), wa2_ref[...].astype(BF16),
                  preferred_element_type=F32) + ba_ref[...]
    log_a = _log_sigmoid(pre) / GLA_TAU
    k = k_ref[...]
    if valid < chunk:
        row_ok = lax.broadcasted_iota(jnp.int32, (chunk, 1), 0) < valid
        log_a = jnp.where(row_ok, log_a, 0.0)
        k = jnp.where(row_ok, k, 0.0)
    ri = lax.broadcasted_iota(jnp.int32, (chunk, chunk), 0)
    ci = lax.broadcasted_iota(jnp.int32, (chunk, chunk), 1)
    causal = (ri >= ci) & ((ri >= sub) == (ci >= sub))
    b = jnp.dot(jnp.where(causal, 1.0, 0.0), log_a, precision=lax.Precision.HIGHEST,
                preferred_element_type=F32)
    b_end = b[chunk - 1:chunk, :]
    q = q_ref[...] * q_scale
    q_t = (q * jnp.exp(b)).astype(BF16)
    k_t = (k * jnp.exp(-b)).astype(BF16)
    if two:
        first = lax.broadcasted_iota(jnp.int32, (chunk, 1), 0) < sub
        b_mid = b[sub - 1:sub, :]
        k_end1 = k * jnp.exp(jnp.where(first, b_mid, b_end) - b)
        q_s0 = (q * jnp.exp(jnp.where(first, b, b + b_mid))).astype(BF16)
        k_end = (k_end1 * jnp.where(first, jnp.exp(b_end), 1.0)).astype(BF16)
        k_end1 = k_end1.astype(BF16)
        b_tot = b_mid + b_end
        cross = (ri >= sub) & (ci < sub)
    else:
        q_s0 = q_t
        k_end = (k * jnp.exp(b_end - b)).astype(BF16)
        b_tot = b_end
    decay_col = jnp.exp(jnp.broadcast_to(b_tot, (LANES, hk)).T[:, :1])
    last = c == pl.num_programs(1) - 1
    nt = (((1,), (1,)), ((), ()))
    for h in range(GLA_HEADS):
        ks = slice(h * dk, (h + 1) * dk)
        vs = slice(h * dv, (h + 1) * dv)
        v = v_ref[:, vs].astype(BF16)
        att = jnp.where(causal, lax.dot_general(q_t[:, ks], k_t[:, ks], nt, preferred_element_type=F32), 0.0)
        if two:
            att = jnp.where(cross, lax.dot_general(q_t[:, ks], k_end1[:, ks], nt,
                                                   preferred_element_type=F32), att)
        s_old = s_ref[h]
        o = (jnp.dot(att.astype(BF16), v, preferred_element_type=F32)
             + jnp.dot(q_s0[:, ks], s_old.astype(BF16), preferred_element_type=F32))
        kv = lax.dot_general(k_end[:, ks], v, (((0,), (0,)), ((), ())), preferred_element_type=F32)
        s_new = decay_col[ks] * s_old + kv
        s_ref[h] = s_new

        @pl.when(last)
        def _(h=h, s_new=s_new):
            sfin_ref[h] = s_new

        oc = o - jnp.mean(o, axis=-1, keepdims=True)
        var = jnp.mean(oc * oc, axis=-1, keepdims=True)
        on = oc * lax.rsqrt(var + LN_EPS) * ng_ref[...]
        r = r_ref[:, vs]
        o_ref[:, vs] = (on * (r * jax.nn.sigmoid(r))).astype(BF16)


def _gla_core(qkvr, glr, w_a2p, b_a, norm_g, s0, n_seq, valid):
    m = qkvr.shape[0]
    hk = w_a2p.shape[-1]
    dk = hk // GLA_HEADS
    dv = norm_g.shape[-1]
    t = m // n_seq
    chunk = next((c for c in (2 * GLA_CHUNK, GLA_CHUNK) if t % c == 0), t)
    nc = t // chunk
    hv = GLA_HEADS * dv
    assert (2 * hk) % hv == 0
    vb0 = 2 * hk // hv
    has_s0 = s0 is not None
    row = lambda b, c: b * nc + c
    state = pl.BlockSpec((None, GLA_HEADS, dk, dv), lambda b, c: (b, 0, 0, 0))
    in_specs = [
        pl.BlockSpec((chunk, hk), lambda b, c: (row(b, c), 0)),
        pl.BlockSpec((chunk, hk), lambda b, c: (row(b, c), 1)),
        pl.BlockSpec((chunk, hv), lambda b, c: (row(b, c), vb0)),
        pl.BlockSpec((chunk, hv), lambda b, c: (row(b, c), vb0 + 1)),
        pl.BlockSpec((chunk, LANES), lambda b, c: (row(b, c), 0)),
        pl.BlockSpec((LANES, hk), lambda b, c: (0, 0)),
        pl.BlockSpec((1, hk), lambda b, c: (0, 0)),
        pl.BlockSpec((1, dv), lambda b, c: (0, 0)),
    ]
    args = [qkvr, qkvr, qkvr, qkvr, glr, w_a2p, b_a.reshape(1, hk), norm_g.reshape(1, dv)]
    if has_s0:
        in_specs.append(state)
        args.append(s0)
    return pl.pallas_call(
        functools.partial(_gla_kernel, chunk=chunk, valid=valid, has_s0=has_s0, q_scale=dk ** -0.5,
                          dk=dk, dv=dv),
        grid=(n_seq, nc),
        in_specs=in_specs,
        out_specs=[pl.BlockSpec((chunk, hv), lambda b, c: (row(b, c), 0)), state],
        out_shape=[jax.ShapeDtypeStruct((m, hv), BF16),
                   jax.ShapeDtypeStruct((n_seq, GLA_HEADS, dk, dv), F32)],
        scratch_shapes=[pltpu.VMEM((GLA_HEADS, dk, dv), F32)],
        compiler_params=_params(("arbitrary", "arbitrary"), 56),
        name="gla_core",
    )(*args)


def _t5_bucket(dist):
    n = jnp.maximum(dist, 0)
    max_exact = REL_BUCKETS // 2
    large = max_exact + (jnp.log(jnp.maximum(n, 1).astype(F32) / max_exact)
                         / math.log(REL_MAX_DIST / max_exact) * (REL_BUCKETS - max_exact)).astype(jnp.int32)
    large = jnp.minimum(large, REL_BUCKETS - 1)
    return jnp.where(n < max_exact, n, large)


def _bias_kernel(qpos_ref, kpos_ref, qgrp_ref, kgrp_ref, rb_ref, o_ref):
    bucket = _t5_bucket(qpos_ref[...] - kpos_ref[...])
    rb = rb_ref[...]
    acc = jnp.zeros(o_ref.shape, F32)
    for i in range(REL_BUCKETS):
        acc = jnp.where(bucket == i, rb[:, i:i + 1], acc)
    o_ref[...] = jnp.where(qgrp_ref[...] == kgrp_ref[...], acc, NEG)


def _bias_table(qpos, kpos, rb_rows, tr, tc, qgrp, kgrp):
    r, c = qpos.shape[0], kpos.shape[0]
    col = lambda a: a.reshape(r, 1).astype(jnp.int32)
    row = lambda a: a.reshape(1, c).astype(jnp.int32)
    return pl.pallas_call(
        _bias_kernel,
        grid=(r // tr, c // tc),
        in_specs=[pl.BlockSpec((tr, 1), lambda i, j: (i, 0)),
                  pl.BlockSpec((1, tc), lambda i, j: (0, j)),
                  pl.BlockSpec((tr, 1), lambda i, j: (i, 0)),
                  pl.BlockSpec((1, tc), lambda i, j: (0, j)),
                  pl.BlockSpec((tr, REL_BUCKETS), lambda i, j: (i, 0))],
        out_specs=pl.BlockSpec((tr, tc), lambda i, j: (i, j)),
        out_shape=jax.ShapeDtypeStruct((r, c), F32),
        compiler_params=_params(("arbitrary", "arbitrary"), 32),
        name="t5_bias",
    )(col(qpos), row(kpos), col(qgrp), row(kgrp), rb_rows)


def _toeplitz_bias_kernel(rb_ref, o_ref, *, t, mult):
    off = pl.program_id(1)
    w = 2 * t
    u = lax.broadcasted_iota(jnp.int32, (1, w), 1)
    dist = off * t + jnp.where(u < t, -u, w - u)
    bucket = _t5_bucket(dist)
    rb = rb_ref[...]
    g = jnp.zeros((1, w), F32)
    for i in range(REL_BUCKETS):
        g = jnp.where(bucket == i, rb[:, i:i + 1], g)
    x = jnp.broadcast_to(g * mult, (t, w))
    o_ref[...] = pltpu.roll(x, 0, 1, stride=1, stride_axis=0)[:, :t]


def _toeplitz_bias(rel_bias, n_off, t, mult):
    n_heads = rel_bias.shape[1]
    return pl.pallas_call(
        functools.partial(_toeplitz_bias_kernel, t=t, mult=mult),
        grid=(n_heads, n_off),
        in_specs=[pl.BlockSpec((None, 1, REL_BUCKETS), lambda h, o: (h, 0, 0))],
        out_specs=pl.BlockSpec((None, None, t, t), lambda h, o: (h, o, 0, 0)),
        out_shape=jax.ShapeDtypeStruct((n_heads, n_off, t, t), F32),
        compiler_params=_params(("arbitrary", "arbitrary"), 32),
        name="t5_bias_tiles",
    )(rel_bias.T.reshape(n_heads, 1, REL_BUCKETS))


def _first_const_dist(limit):
    n = np.arange(1, limit + 1)
    large = 16 + (np.log(n.astype(np.float32) / np.float32(16)) / np.float32(math.log(8.0))
                  * np.float32(16)).astype(np.int32)
    bucket = np.where(n < 16, n, np.minimum(large, REL_BUCKETS - 1))
    not_last = np.nonzero(bucket != REL_BUCKETS - 1)[0]
    return int(n[not_last[-1]] + 1) if not_last.size else 1


def _lambda(lam_ref, lam_init):
    lp = lam_ref[...]
    a = jnp.sum(lp[0:1] * lp[1:2], axis=-1, keepdims=True)
    b = jnp.sum(lp[2:3] * lp[3:4], axis=-1, keepdims=True)
    return jnp.exp(a) - jnp.exp(b) + lam_init


def _flash_kernel(q_ref, k_ref, v_ref, bias_ref, lam_ref, ng_ref, o_ref,
                  kb_ref, vb_ref, m_ref, l_ref, acc_ref, *, t, hd, n_near, lam_init, scale):
    qi = pl.program_id(2)

    @pl.when(qi == 0)
    def _():
        kb_ref[...] = k_ref[...].astype(BF16)
        vb_ref[...] = v_ref[...].astype(BF16)

    q = q_ref[...] * (scale * LOG2E)
    lane = lax.broadcasted_iota(jnp.int32, q.shape, 1)
    q2 = jnp.concatenate([jnp.where(lane < hd, q, 0.0), jnp.where(lane >= hd, q, 0.0)],
                         axis=0).astype(BF16)
    m_ref[...] = jnp.full(m_ref.shape, NEG, F32)
    l_ref[...] = jnp.zeros(l_ref.shape, F32)
    acc_ref[...] = jnp.zeros(acc_ref.shape, F32)
    far_bias = bias_ref[n_near - 1, t - 1:t, 0:1]

    def tile(kj, bias, mask):
        rows = pl.ds(pl.multiple_of(kj * t, t), t)
        kt = kb_ref[rows, :]
        vt = vb_ref[rows, :]
        for c in range(2):
            rc = slice(c * t, (c + 1) * t)
            s = lax.dot_general(q2[rc], kt, (((1,), (1,)), ((), ())),
                                preferred_element_type=F32)
            s = s + bias
            if mask is not None:
                s = jnp.where(mask, s, NEG)
            m_old = m_ref[rc, :]
            m_new = jnp.maximum(m_old, jnp.max(s, axis=-1, keepdims=True))
            alpha = jnp.exp2(m_old - m_new)
            p = jnp.exp2(s - pltpu.repeat(m_new, t // LANES, axis=1))
            l_ref[rc, :] = alpha * l_ref[rc, :] + jnp.sum(p, axis=-1, keepdims=True)
            acc_ref[rc, :] = (pltpu.repeat(alpha, 2 * hd // LANES, axis=1) * acc_ref[rc, :]
                              + jnp.dot(p.astype(BF16), vt, preferred_element_type=F32))
            m_ref[rc, :] = m_new

    def far_body(kj, carry):
        tile(kj, far_bias, None)
        return carry

    lax.fori_loop(0, jnp.maximum(qi - (n_near - 1), 0), far_body, 0)
    for off in range(n_near - 1, 0, -1):
        @pl.when(qi >= off)
        def _(off=off):
            tile(qi - off, bias_ref[off], None)
    ri = lax.broadcasted_iota(jnp.int32, (t, t), 0)
    ci = lax.broadcasted_iota(jnp.int32, (t, t), 1)
    tile(qi, bias_ref[0], ri >= ci)

    lam = _lambda(lam_ref, lam_init)
    a = acc_ref[...] / pltpu.repeat(l_ref[...], 2 * hd // LANES, axis=1)
    o = a[:t] - lam * a[t:]
    y = o * lax.rsqrt(jnp.mean(o * o, axis=-1, keepdims=True) + LN_EPS) * ng_ref[...]
    o_ref[...] = (y * (1.0 - lam_init)).astype(BF16)


def _flash_diff(q, k, v, rel_bias, lam_params, norm_g, n_seq, lam_init):
    m, d = q.shape
    t_seq = m // n_seq
    hw = d // DIFF_HEADS
    hd = hw // 2
    t = next(c for c in range(FLASH_TILE, 0, -LANES) if t_seq % c == 0)
    nq = t_seq // t
    n_near = min(nq, -(-(_first_const_dist(t_seq) - 1) // t) + 1)
    bias = _toeplitz_bias(rel_bias, n_near, t, LOG2E)
    return pl.pallas_call(
        functools.partial(_flash_kernel, t=t, hd=hd, n_near=n_near, lam_init=lam_init, scale=hd ** -0.5),
        grid=(n_seq, DIFF_HEADS, nq),
        in_specs=[pl.BlockSpec((t, hw), lambda b, h, i: (b * nq + i, h)),
                  pl.BlockSpec((t_seq, hw), lambda b, h, i: (b, h)),
                  pl.BlockSpec((t_seq, hw), lambda b, h, i: (b, h)),
                  pl.BlockSpec((None, n_near, t, t), lambda b, h, i: (h, 0, 0, 0)),
                  pl.BlockSpec((4, hd), lambda b, h, i: (0, 0)),
                  pl.BlockSpec((1, hw), lambda b, h, i: (0, 0))],
        out_specs=pl.BlockSpec((t, hw), lambda b, h, i: (b * nq + i, h)),
        out_shape=jax.ShapeDtypeStruct((m, d), BF16),
        scratch_shapes=[pltpu.VMEM((t_seq, hw), BF16), pltpu.VMEM((t_seq, hw), BF16),
                        pltpu.VMEM((2 * t, LANES), F32), pltpu.VMEM((2 * t, LANES), F32),
                        pltpu.VMEM((2 * t, hw), F32)],
        compiler_params=_params(("arbitrary", "arbitrary", "arbitrary"), 48),
        name="flash_diff",
    )(q, k, v, bias, lam_params, norm_g.reshape(1, hw))


def _paged_kernel(pt_ref, q_ref, kn_ref, vn_ref, tab_ref, tabn_ref, lam_ref, ng_ref, *rest,
                  pages, q_tok, lam_init, scale):
    k_refs = rest[:pages]
    v_refs = rest[pages:2 * pages]
    o_ref, m_ref, l_ref, acc_ref = rest[2 * pages:]
    s_id = pl.program_id(1)
    rows, hw = q_ref.shape
    half = rows // 2
    q = q_ref[...]

    def scores(kmat):
        return lax.dot_general(q, kmat, (((1,), (1,)), ((), ())), preferred_element_type=F32) * scale

    def update(s, vmat):
        m_old = m_ref[...]
        m_new = jnp.maximum(m_old, jnp.max(s, axis=-1, keepdims=True))
        alpha = jnp.exp(m_old - m_new)
        p = jnp.exp(s - pltpu.repeat(m_new, s.shape[1] // LANES, axis=1))
        l_ref[...] = alpha * l_ref[...] + jnp.sum(p, axis=-1, keepdims=True)
        acc_ref[...] = (pltpu.repeat(alpha, hw // LANES, axis=1) * acc_ref[...]
                        + jnp.dot(p.astype(BF16), vmat, preferred_element_type=F32))
        m_ref[...] = m_new

    @pl.when(s_id == 0)
    def _():
        m_ref[...] = jnp.full(m_ref.shape, NEG, F32)
        l_ref[...] = jnp.zeros(l_ref.shape, F32)
        acc_ref[...] = jnp.zeros(acc_ref.shape, F32)
        s = scores(kn_ref[...].astype(BF16)) + tabn_ref[...]
        tq = lax.broadcasted_iota(jnp.int32, s.shape, 0) % q_tok
        tk = lax.broadcasted_iota(jnp.int32, s.shape, 1) // DIFF_HEADS
        update(jnp.where(tk <= tq, s, NEG), vn_ref[...].astype(BF16))

    kp = jnp.concatenate([r[...].astype(BF16) for r in k_refs], axis=0)
    vp = jnp.concatenate([r[...].astype(BF16) for r in v_refs], axis=0)
    update(scores(kp) + tab_ref[...], vp)

    @pl.when(s_id == pl.num_programs(1) - 1)
    def _():
        a = acc_ref[...] / pltpu.repeat(l_ref[...], hw // LANES, axis=1)
        lam = _lambda(lam_ref, lam_init)
        o = a[:half] - lam * a[half:]
        y = o * lax.rsqrt(jnp.mean(o * o, axis=-1, keepdims=True) + LN_EPS) * ng_ref[...]
        o_ref[...] = (y * (1.0 - lam_init)).astype(o_ref.dtype)


def _paged_diff(q, k_new, v_new, cache_k, cache_v, page_table, layer, rel_bias, lam_params, norm_g,
                lam_init):
    nb, tq, d = q.shape
    tk = k_new.shape[1]
    assert tq <= tk
    hw = d // DIFF_HEADS
    hd = hw // 2
    n_pages = page_table.shape[1]
    past = n_pages * PAGE_SIZE
    pages = next(p for p in (PAGES_PER_STEP, 2, 1) if n_pages % p == 0)
    n_steps = n_pages // pages
    rows = 2 * DIFF_HEADS * tq
    page_rows = PAGE_SIZE * DIFF_HEADS
    assert pages * PAGE_SIZE + 1 >= _first_const_dist(past + tq)
    q5 = q.reshape(nb, tq, DIFF_HEADS, 2, hd).transpose(0, 3, 2, 1, 4)
    same_c = jnp.arange(2)[:, None] == jnp.arange(2)[None, :]
    qd = jnp.where(same_c[None, :, None, None, :, None], q5[:, :, :, :, None, :], 0.0)
    qd = qd.reshape(nb, rows, hw).astype(BF16)
    k_rows = k_new.reshape(nb, tk * DIFF_HEADS, hw)
    v_rows = v_new.reshape(nb, tk * DIFF_HEADS, hw)
    qpos = jnp.tile(past + jnp.arange(tq), 2 * DIFF_HEADS)
    qgrp = jnp.tile(jnp.repeat(jnp.arange(DIFF_HEADS), tq), 2)
    rb_rows = jnp.tile(jnp.repeat(rel_bias.T, tq, axis=0), (2, 1))
    col_tok = jnp.repeat(jnp.arange(pages * PAGE_SIZE), DIFF_HEADS)
    col_grp = jnp.tile(jnp.arange(DIFF_HEADS), pages * PAGE_SIZE)
    kpos = jnp.concatenate([jnp.zeros_like(col_tok), (n_steps - 1) * pages * PAGE_SIZE + col_tok])
    tab = _bias_table(qpos, kpos, rb_rows, rows, page_rows, qgrp, jnp.tile(col_grp, 2))
    tab = tab.reshape(rows, 2, pages * page_rows).transpose(1, 0, 2)
    tab_new = _bias_table(qpos, past + jnp.repeat(jnp.arange(tk), DIFF_HEADS), rb_rows, rows,
                          tk * DIFF_HEADS, qgrp, jnp.tile(jnp.arange(DIFF_HEADS), tk))
    ck = cache_k.reshape(cache_k.shape[0], cache_k.shape[1], page_rows, hw)
    cv = cache_v.reshape(cache_v.shape[0], cache_v.shape[1], page_rows, hw)

    def page_spec(i):
        return pl.BlockSpec((None, None, page_rows, hw),
                            lambda b, s, pt: (layer, pt[b, s * pages + i], 0, 0))

    grid_spec = pltpu.PrefetchScalarGridSpec(
        num_scalar_prefetch=1,
        grid=(nb, n_steps),
        in_specs=[pl.BlockSpec((None, rows, hw), lambda b, s, pt: (b, 0, 0)),
                  pl.BlockSpec((None, tk * DIFF_HEADS, hw), lambda b, s, pt: (b, 0, 0)),
                  pl.BlockSpec((None, tk * DIFF_HEADS, hw), lambda b, s, pt: (b, 0, 0)),
                  pl.BlockSpec((None, rows, pages * page_rows),
                               lambda b, s, pt: (jnp.where(s == n_steps - 1, 1, 0), 0, 0)),
                  pl.BlockSpec((rows, tk * DIFF_HEADS), lambda b, s, pt: (0, 0)),
                  pl.BlockSpec((4, hd), lambda b, s, pt: (0, 0)),
                  pl.BlockSpec((1, hw), lambda b, s, pt: (0, 0))]
                 + [page_spec(i) for i in range(pages)] + [page_spec(i) for i in range(pages)],
        out_specs=pl.BlockSpec((None, rows // 2, hw), lambda b, s, pt: (b, 0, 0)),
        scratch_shapes=[pltpu.VMEM((rows, LANES), F32), pltpu.VMEM((rows, LANES), F32),
                        pltpu.VMEM((rows, hw), F32)],
    )
    out = pl.pallas_call(
        functools.partial(_paged_kernel, pages=pages, q_tok=tq, lam_init=lam_init, scale=hd ** -0.5),
        grid_spec=grid_spec,
        out_shape=jax.ShapeDtypeStruct((nb, rows // 2, hw), BF16),
        compiler_params=_params(("arbitrary", "arbitrary"), 56),
        name="paged_diff",
    )(page_table, qd, k_rows, v_rows, tab, tab_new, lam_params, norm_g.reshape(1, hw),
      *([ck] * pages), *([cv] * pages))
    return out.reshape(nb, DIFF_HEADS, tq, hw).transpose(0, 2, 1, 3).reshape(nb, tq, d)


def kernel(x_prompt, x_sample, state_gla, cache_k, cache_v, page_table, ln_g, ln_b, ffn_w_gate, ffn_w_up,
           ffn_w_down, gla_w_in, gla_w_a2, gla_b_a, gla_norm_g, gla_w_o, diff_w_in, diff_lambda, diff_norm_g,
           diff_w_o, rel_bias):
    nb, t, d = x_prompt.shape
    sb, st, _ = x_sample.shape
    hw = d // DIFF_HEADS
    assert st <= SAMPLE_KEY_ROWS <= SAMPLE_ROWS
    xs = jnp.pad(x_sample, ((0, 0), (0, SAMPLE_ROWS - st), (0, 0))).reshape(sb * SAMPLE_ROWS, d)
    x0 = (x_prompt.reshape(nb * t, d), xs)
    xf = ("plain", x0)
    xb = tuple(_to_bf16(a) for a in x0)
    gla_p, gla_s, k_rows, v_rows = [], [], [], []

    def layer_norm(z, g, b):
        outs = [_layer_norm(a, g, b) for a in z]
        xb, mean, rstd = (tuple(o[i] for o in outs) for i in range(3))
        return ("ln", z, mean, rstd, g, b), xb

    def ffn(xf, xb, i, s):
        h = _ffn_up(xb, ffn_w_gate, ffn_w_up, (i, s))
        z = _res_mm(h, ffn_w_down, (i, s), xf, 0.5)
        if (i, s) == (DEPTH - 1, 1):
            return z, None
        return layer_norm(z, ln_g[i, 2 * s], ln_b[i, 2 * s])

    def sample_seq(a, rows):
        return a.reshape(sb, SAMPLE_ROWS, a.shape[-1])[:, :rows]

    def sample_rows(a):
        return jnp.pad(a, ((0, 0), (0, SAMPLE_ROWS - a.shape[1]), (0, 0))).reshape(sb * SAMPLE_ROWS, a.shape[-1])

    for i in range(DEPTH):
        xf, xb = ffn(xf, xb, i, 0)
        j = i // N_MIXERS
        if i % N_MIXERS == 0:
            hk = gla_w_a2.shape[-1]
            hv = gla_w_o.shape[1]
            qkvr = _proj(xb, gla_w_in, (j,), 0, 2 * hk + 2 * hv, 512)
            w_g = jnp.pad(gla_w_in[j][:, 2 * hk + 2 * hv:], ((0, 0), (0, LANES - GLA_GATE_RANK)))
            glr = _proj(xb, w_g[None], (0,), 0, LANES, LANES)
            w_a2p = jnp.pad(gla_w_a2[j], ((0, LANES - GLA_GATE_RANK), (0, 0)))
            o_p, s_p = _gla_core(qkvr[0], glr[0], w_a2p, gla_b_a[j], gla_norm_g[j], None, nb, t)
            chunk = lambda a: jnp.pad(sample_seq(a, SAMPLE_ROWS), ((0, 0), (0, GLA_CHUNK - SAMPLE_ROWS), (0, 0))
                                      ).reshape(sb * GLA_CHUNK, a.shape[-1])
            o_s, s_s = _gla_core(chunk(qkvr[1]), chunk(glr[1]), w_a2p, gla_b_a[j], gla_norm_g[j],
                                 state_gla[j], sb, st)
            o_s = o_s.reshape(sb, GLA_CHUNK, hv)[:, :SAMPLE_ROWS].reshape(sb * SAMPLE_ROWS, hv)
            gla_p.append(s_p)
            gla_s.append(s_s)
            z = _res_mm((o_p, o_s), gla_w_o, (j,), xf, 1.0)
        else:
            lam_init = 0.8 - 0.6 * math.exp(-0.3 * i)
            q = _proj(xb, diff_w_in, (j,), 0, d, 512)
            k = _proj(xb, diff_w_in, (j,), d // 512, d, 512)
            v = _proj(xb, diff_w_in, (j,), 2 * d // 512, d, 512)
            k_rows.append(k)
            v_rows.append(v)
            o_p = _flash_diff(q[0], k[0], v[0], rel_bias, diff_lambda[j], diff_norm_g[j], nb, lam_init)
            o_s = _paged_diff(sample_seq(q[1], st), sample_seq(k[1], SAMPLE_KEY_ROWS),
                              sample_seq(v[1], SAMPLE_KEY_ROWS), cache_k, cache_v, page_table, j, rel_bias,
                              diff_lambda[j], diff_norm_g[j], lam_init)
            z = _res_mm((o_p, sample_rows(o_s)), diff_w_o, (j,), xf, 1.0)
        xf, xb = layer_norm(z, ln_g[i, 1], ln_b[i, 1])
        xf, xb = ffn(xf, xb, i, 1)

    y = [_layer_norm(a, ln_g[DEPTH - 1, 2], ln_b[DEPTH - 1, 2], final=True)[0] for a in xf]
    heads = lambda a, n_seq, rows: a.reshape(n_seq, rows, DIFF_HEADS, hw)
    return (y[0].reshape(nb, t, d),
            sample_seq(y[1], st),
            jnp.stack(gla_p),
            jnp.stack(gla_s),
            jnp.stack([heads(k[0], nb, t) for k in k_rows]),
            jnp.stack([heads(v[0], nb, t) for v in v_rows]),
            jnp.stack([heads(sample_seq(k[1], st), sb, st) for k in k_rows]),
            jnp.stack([heads(sample_seq(v[1], st), sb, st) for v in v_rows]))
```

```python
import functools
import math

import jax
import jax.numpy as jnp
import numpy as np
from jax import lax
from jax.experimental import pallas as pl
from jax.experimental.pallas import tpu as pltpu

F32 = jnp.float32
BF16 = jnp.bfloat16

DEPTH = 2
N_MIXERS = 2
GLA_HEADS = 4
GLA_GATE_RANK = 16
GLA_TAU = 16.0
GLA_CHUNK = 64
DIFF_HEADS = 16
PAGE_SIZE = 128
REL_BUCKETS = 32
REL_MAX_DIST = 128
DEEPNORM_ALPHA = (2.0 * DEPTH) ** 0.25
LN_EPS = 1e-5
NEG = -1e30
LOG2E = 1.0 / math.log(2.0)

LANES = 128
MIB = 1024 * 1024
SPLIT_K_ABOVE = 8192
FLASH_TILE = 512
PAGES_PER_STEP = 4
SAMPLE_ROWS = 8
SAMPLE_KEY_ROWS = 8


def _row_tile(m, cap):
    for t in range(min(m, cap) // 16 * 16, 15, -16):
        if m % t == 0:
            return t
    raise ValueError(f"no row tile for {m} rows")


def _params(sem, vmem_mib):
    return pltpu.CompilerParams(dimension_semantics=sem, vmem_limit_bytes=vmem_mib * MIB)


def _ln_apply(z, mean, rstd, g, b):
    return (z - mean) * rstd * g + b


def _ln_kernel(z_ref, g_ref, b_ref, *out_refs, final):
    z = z_ref[...]
    mean = jnp.mean(z, axis=-1, keepdims=True)
    zc = z - mean
    rstd = lax.rsqrt(jnp.mean(zc * zc, axis=-1, keepdims=True) + LN_EPS)
    y = _ln_apply(z, mean, rstd, g_ref[...], b_ref[...])
    if final:
        out_refs[0][...] = y
    else:
        xb_ref, mean_ref, rstd_ref = out_refs
        xb_ref[...] = y.astype(BF16)
        mean_ref[...] = jnp.broadcast_to(mean, mean_ref.shape)
        rstd_ref[...] = jnp.broadcast_to(rstd, rstd_ref.shape)


def _layer_norm(z, g, b, final=False):
    m, d = z.shape
    tr = _row_tile(m, 256)
    rows = lambda w: pl.BlockSpec((tr, w), lambda i: (i, 0))
    if final:
        out_specs, out_shape = [rows(d)], [jax.ShapeDtypeStruct((m, d), F32)]
    else:
        out_specs = [rows(d), rows(LANES), rows(LANES)]
        out_shape = [jax.ShapeDtypeStruct((m, d), BF16), jax.ShapeDtypeStruct((m, LANES), F32),
                     jax.ShapeDtypeStruct((m, LANES), F32)]
    return pl.pallas_call(
        functools.partial(_ln_kernel, final=final),
        grid=(m // tr,),
        in_specs=[rows(d), pl.BlockSpec((1, d), lambda i: (0, 0)), pl.BlockSpec((1, d), lambda i: (0, 0))],
        out_specs=out_specs,
        out_shape=out_shape,
        compiler_params=_params(("arbitrary",), 40),
        name="layer_norm",
    )(z, g.reshape(1, d), b.reshape(1, d))


def _cast_kernel(x_ref, o_ref):
    o_ref[...] = x_ref[...].astype(BF16)


def _to_bf16(x):
    m, d = x.shape
    tr = _row_tile(m, 256)
    return pl.pallas_call(
        _cast_kernel,
        grid=(m // tr,),
        in_specs=[pl.BlockSpec((tr, d), lambda i: (i, 0))],
        out_specs=pl.BlockSpec((tr, d), lambda i: (i, 0)),
        out_shape=jax.ShapeDtypeStruct((m, d), BF16),
        compiler_params=_params(("arbitrary",), 40),
        name="to_bf16",
    )(x)


def _w_spec(w, lead, k, tn, col0):
    none = (None,) * len(lead)
    return pl.BlockSpec(none + (k, tn), lambda n, m: tuple(lead) + (0, col0 + n))


def _row_specs(tm, ms, width, col):
    return [pl.BlockSpec((tm, width), lambda n, i: (i, col(n))),
            pl.BlockSpec((ms, width), lambda n, i: (0, col(n)))]


def _pair_shapes(m, ms, n_cols, dtype):
    return [jax.ShapeDtypeStruct((m, n_cols), dtype), jax.ShapeDtypeStruct((ms, n_cols), dtype)]


def _proj_kernel(x_ref, xs_ref, w_ref, o_ref, os_ref, wb_ref):
    @pl.when(pl.program_id(1) == 0)
    def _():
        wb_ref[...] = w_ref[...].astype(BF16)
        os_ref[...] = jnp.dot(xs_ref[...], wb_ref[...], preferred_element_type=F32).astype(os_ref.dtype)

    o_ref[...] = jnp.dot(x_ref[...], wb_ref[...], preferred_element_type=F32).astype(o_ref.dtype)


def _proj(x, w, lead, col0, n_cols, tn, out_dtype=F32):
    (m, k), ms = x[0].shape, x[1].shape[0]
    tm = _row_tile(m, 1024)
    return pl.pallas_call(
        _proj_kernel,
        grid=(n_cols // tn, m // tm),
        in_specs=_row_specs(tm, ms, k, lambda n: 0) + [_w_spec(w, lead, k, tn, col0)],
        out_specs=_row_specs(tm, ms, tn, lambda n: n),
        out_shape=_pair_shapes(m, ms, n_cols, out_dtype),
        scratch_shapes=[pltpu.VMEM((k, tn), BF16)],
        compiler_params=_params(("arbitrary", "arbitrary"), 56),
        name="proj",
    )(x[0], x[1], w)


def _ffn_up_kernel(x_ref, x0s_ref, wg_ref, wu_ref, h_ref, hs_ref, wgb_ref, wub_ref):
    tm = x_ref.shape[0]

    def swiglu(x):
        g = jnp.dot(x, wgb_ref[...], preferred_element_type=F32)
        u = jnp.dot(x, wub_ref[...], preferred_element_type=F32)
        return (g * jax.nn.sigmoid(g) * u).astype(BF16)

    @pl.when(pl.program_id(1) == 0)
    def _():
        wgb_ref[...] = wg_ref[...].astype(BF16)
        wub_ref[...] = wu_ref[...].astype(BF16)
        y = swiglu(x0s_ref[...])
        h_ref[...] = y[:tm]
        hs_ref[...] = y[tm:]

    @pl.when(pl.program_id(1) > 0)
    def _():
        h_ref[...] = swiglu(x_ref[...])


def _ffn_up(x, w_gate, w_up, lead):
    (m, k), ms = x[0].shape, x[1].shape[0]
    f = w_gate.shape[-1]
    tm = _row_tile(m, 1024)
    tn = 256
    x0s = jnp.concatenate([x[0][:tm], x[1]], axis=0)
    return pl.pallas_call(
        _ffn_up_kernel,
        grid=(f // tn, m // tm),
        in_specs=[pl.BlockSpec((tm, k), lambda n, i: (i, 0)),
                  pl.BlockSpec((tm + ms, k), lambda n, i: (0, 0), pipeline_mode=pl.Buffered(1)),
                  _w_spec(w_gate, lead, k, tn, 0), _w_spec(w_up, lead, k, tn, 0)],
        out_specs=_row_specs(tm, ms, tn, lambda n: n),
        out_shape=_pair_shapes(m, ms, f, BF16),
        scratch_shapes=[pltpu.VMEM((k, tn), BF16), pltpu.VMEM((k, tn), BF16)],
        compiler_params=_params(("arbitrary", "arbitrary"), 56),
        name="ffn_up",
    )(x[0], x0s, w_gate, w_up)


def _res_mm_kernel(*refs, alpha, scale, has_partial, res_mode):
    h_ref, hs_ref, w_ref = refs[:3]
    o_ref, os_ref, wb_ref = refs[-3:]
    extra = list(refs[3:-3])
    partial = [extra.pop(0), extra.pop(0)] if has_partial else None

    def residual(group):
        if res_mode == "plain":
            return extra[group][...]
        z, mean, rstd = extra[group], extra[2 + group], extra[4 + group]
        wide = lambda a: pltpu.repeat(a[...], z.shape[1] // LANES, axis=1)
        return _ln_apply(z[...], wide(mean), wide(rstd), extra[6][...], extra[7][...])

    def epilogue(y, group):
        if has_partial:
            y = partial[group][...] + y
        if res_mode is not None:
            y = alpha * residual(group) + scale * y
        return y

    @pl.when(pl.program_id(1) == 0)
    def _():
        wb_ref[...] = w_ref[...].astype(BF16)
        os_ref[...] = epilogue(jnp.dot(hs_ref[...], wb_ref[...], preferred_element_type=F32), 1)

    o_ref[...] = epilogue(jnp.dot(h_ref[...], wb_ref[...], preferred_element_type=F32), 0)


def _res_mm_pass(h, w, lead, kb, n_kb, partial, res, scale, tm_cap):
    (m, k), ms = h[0].shape, h[1].shape[0]
    kh = k // n_kb
    n_cols = w.shape[-1]
    tn = 512
    tm = _row_tile(m, tm_cap)
    none = (None,) * len(lead)
    tiles = lambda: _row_specs(tm, ms, tn, lambda n: n)
    in_specs = _row_specs(tm, ms, kh, lambda n: kb) + [
        pl.BlockSpec(none + (kh, tn), lambda n, i: tuple(lead) + (kb, n))]
    args = [h[0], h[1], w]
    if partial is not None:
        in_specs += tiles()
        args += list(partial)
    if res is not None and res[0] == "plain":
        in_specs += tiles()
        args += list(res[1])
    elif res is not None:
        _, z, mean, rstd, g, b = res
        vec = pl.BlockSpec((1, tn), lambda n, i: (0, n))
        in_specs += tiles() + 2 * _row_specs(tm, ms, LANES, lambda n: 0) + [vec, vec]
        args += list(z) + list(mean) + list(rstd) + [g.reshape(1, n_cols), b.reshape(1, n_cols)]
    return pl.pallas_call(
        functools.partial(_res_mm_kernel, alpha=DEEPNORM_ALPHA, scale=scale,
                          has_partial=partial is not None, res_mode=None if res is None else res[0]),
        grid=(n_cols // tn, m // tm),
        in_specs=in_specs,
        out_specs=_row_specs(tm, ms, tn, lambda n: n),
        out_shape=_pair_shapes(m, ms, n_cols, F32),
        scratch_shapes=[pltpu.VMEM((kh, tn), BF16)],
        compiler_params=_params(("arbitrary", "arbitrary"), 56),
        name="res_mm",
    )(*args)


def _res_mm(h, w, lead, res, scale):
    k = h[0].shape[1]
    if k > SPLIT_K_ABOVE and (k // 2) % LANES == 0:
        part = _res_mm_pass(h, w, lead, 0, 2, None, None, scale, 512)
        return _res_mm_pass(h, w, lead, 1, 2, part, res, scale, 512)
    return _res_mm_pass(h, w, lead, 0, 1, None, res, scale, 1024)


def _log_sigmoid(x):
    return jnp.minimum(x, 0.0) - jnp.log(1.0 + jnp.exp(-jnp.abs(x)))


def _gla_kernel(*refs, chunk, valid, has_s0, q_scale, dk, dv):
    if has_s0:
        (q_ref, k_ref, v_ref, r_ref, glr_ref, wa2_ref, ba_ref, ng_ref, s0_ref,
         o_ref, sfin_ref, s_ref) = refs
    else:
        (q_ref, k_ref, v_ref, r_ref, glr_ref, wa2_ref, ba_ref, ng_ref,
         o_ref, sfin_ref, s_ref) = refs
    c = pl.program_id(1)

    @pl.when(c == 0)
    def _():
        if has_s0:
            s_ref[...] = s0_ref[...]
        else:
            s_ref[...] = jnp.zeros_like(s_ref)

    hk = q_ref.shape[-1]
    sub = GLA_CHUNK if chunk % GLA_CHUNK == 0 else chunk
    assert chunk in (sub, 2 * sub)
    two = chunk == 2 * sub
    pre = jnp.dot(glr_ref[...].astype(BF16), wa2_ref[...].astype(BF16),
                  preferred_element_type=F32) + ba_ref[...]
    log_a = _log_sigmoid(pre) / GLA_TAU
    k = k_ref[...]
    if valid < chunk:
        row_ok = lax.broadcasted_iota(jnp.int32, (chunk, 1), 0) < valid
        log_a = jnp.where(row_ok, log_a, 0.0)
        k = jnp.where(row_ok, k, 0.0)
    ri = lax.broadcasted_iota(jnp.int32, (chunk, chunk), 0)
    ci = lax.broadcasted_iota(jnp.int32, (chunk, chunk), 1)
    causal = (ri >= ci) & ((ri >= sub) == (ci >= sub))
    b = jnp.dot(jnp.where(causal, 1.0, 0.0), log_a, precision=lax.Precision.HIGHEST,
                preferred_element_type=F32)
    b_end = b[chunk - 1:chunk, :]
    q = q_ref[...] * q_scale
    q_t = (q * jnp.exp(b)).astype(BF16)
    k_t = (k * jnp.exp(-b)).astype(BF16)
    if two:
        first = lax.broadcasted_iota(jnp.int32, (chunk, 1), 0) < sub
        b_mid = b[sub - 1:sub, :]
        k_end1 = k * jnp.exp(jnp.where(first, b_mid, b_end) - b)
        q_s0 = (q * jnp.exp(jnp.where(first, b, b + b_mid))).astype(BF16)
        k_end = (k_end1 * jnp.where(first, jnp.exp(b_end), 1.0)).astype(BF16)
        k_end1 = k_end1.astype(BF16)
        b_tot = b_mid + b_end
        cross = (ri >= sub) & (ci < sub)
    else:
        q_s0 = q_t
        k_end = (k * jnp.exp(b_end - b)).astype(BF16)
        b_tot = b_end
    decay_col = jnp.exp(jnp.broadcast_to(b_tot, (LANES, hk)).T[:, :1])
    last = c == pl.num_programs(1) - 1
    nt = (((1,), (1,)), ((), ()))
    for h in range(GLA_HEADS):
        ks = slice(h * dk, (h + 1) * dk)
        vs = slice(h * dv, (h + 1) * dv)
        v = v_ref[:, vs].astype(BF16)
        att = jnp.where(causal, lax.dot_general(q_t[:, ks], k_t[:, ks], nt, preferred_element_type=F32), 0.0)
        if two:
            att = jnp.where(cross, lax.dot_general(q_t[:, ks], k_end1[:, ks], nt,
                                                   preferred_element_type=F32), att)
        s_old = s_ref[h]
        o = (jnp.dot(att.astype(BF16), v, preferred_element_type=F32)
             + jnp.dot(q_s0[:, ks], s_old.astype(BF16), preferred_element_type=F32))
        kv = lax.dot_general(k_end[:, ks], v, (((0,), (0,)), ((), ())), preferred_element_type=F32)
        s_new = decay_col[ks] * s_old + kv
        s_ref[h] = s_new

        @pl.when(last)
        def _(h=h, s_new=s_new):
            sfin_ref[h] = s_new

        oc = o - jnp.mean(o, axis=-1, keepdims=True)
        var = jnp.mean(oc * oc, axis=-1, keepdims=True)
        on = oc * lax.rsqrt(var + LN_EPS) * ng_ref[...]
        r = r_ref[:, vs]
        o_ref[:, vs] = (on * (r * jax.nn.sigmoid(r))).astype(BF16)


def _gla_core(qkvr, glr, w_a2p, b_a, norm_g, s0, n_seq, valid):
    m = qkvr.shape[0]
    hk = w_a2p.shape[-1]
    dk = hk // GLA_HEADS
    dv = norm_g.shape[-1]
    t = m // n_seq
    chunk = next((c for c in (2 * GLA_CHUNK, GLA_CHUNK) if t % c == 0), t)
    nc = t // chunk
    hv = GLA_HEADS * dv
    assert (2 * hk) % hv == 0
    vb0 = 2 * hk // hv
    has_s0 = s0 is not None
    row = lambda b, c: b * nc + c
    state = pl.BlockSpec((None, GLA_HEADS, dk, dv), lambda b, c: (b, 0, 0, 0))
    in_specs = [
        pl.BlockSpec((chunk, hk), lambda b, c: (row(b, c), 0)),
        pl.BlockSpec((chunk, hk), lambda b, c: (row(b, c), 1)),
        pl.BlockSpec((chunk, hv), lambda b, c: (row(b, c), vb0)),
        pl.BlockSpec((chunk, hv), lambda b, c: (row(b, c), vb0 + 1)),
        pl.BlockSpec((chunk, LANES), lambda b, c: (row(b, c), 0)),
        pl.BlockSpec((LANES, hk), lambda b, c: (0, 0)),
        pl.BlockSpec((1, hk), lambda b, c: (0, 0)),
        pl.BlockSpec((1, dv), lambda b, c: (0, 0)),
    ]
    args = [qkvr, qkvr, qkvr, qkvr, glr, w_a2p, b_a.reshape(1, hk), norm_g.reshape(1, dv)]
    if has_s0:
        in_specs.append(state)
        args.append(s0)
    return pl.pallas_call(
        functools.partial(_gla_kernel, chunk=chunk, valid=valid, has_s0=has_s0, q_scale=dk ** -0.5,
                          dk=dk, dv=dv),
        grid=(n_seq, nc),
        in_specs=in_specs,
        out_specs=[pl.BlockSpec((chunk, hv), lambda b, c: (row(b, c), 0)), state],
        out_shape=[jax.ShapeDtypeStruct((m, hv), BF16),
                   jax.ShapeDtypeStruct((n_seq, GLA_HEADS, dk, dv), F32)],
        scratch_shapes=[pltpu.VMEM((GLA_HEADS, dk, dv), F32)],
        compiler_params=_params(("arbitrary", "arbitrary"), 56),
        name="gla_core",
    )(*args)


def _t5_bucket(dist):
    n = jnp.maximum(dist, 0)
    max_exact = REL_BUCKETS // 2
    large = max_exact + (jnp.log(jnp.maximum(n, 1).astype(F32) / max_exact)
                         / math.log(REL_MAX_DIST / max_exact) * (REL_BUCKETS - max_exact)).astype(jnp.int32)
    large = jnp.minimum(large, REL_BUCKETS - 1)
    return jnp.where(n < max_exact, n, large)


def _bias_kernel(qpos_ref, kpos_ref, qgrp_ref, kgrp_ref, rb_ref, o_ref):
    bucket = _t5_bucket(qpos_ref[...] - kpos_ref[...])
    rb = rb_ref[...]
    acc = jnp.zeros(o_ref.shape, F32)
    for i in range(REL_BUCKETS):
        acc = jnp.where(bucket == i, rb[:, i:i + 1], acc)
    o_ref[...] = jnp.where(qgrp_ref[...] == kgrp_ref[...], acc, NEG)


def _bias_table(qpos, kpos, rb_rows, tr, tc, qgrp, kgrp):
    r, c = qpos.shape[0], kpos.shape[0]
    col = lambda a: a.reshape(r, 1).astype(jnp.int32)
    row = lambda a: a.reshape(1, c).astype(jnp.int32)
    return pl.pallas_call(
        _bias_kernel,
        grid=(r // tr, c // tc),
        in_specs=[pl.BlockSpec((tr, 1), lambda i, j: (i, 0)),
                  pl.BlockSpec((1, tc), lambda i, j: (0, j)),
                  pl.BlockSpec((tr, 1), lambda i, j: (i, 0)),
                  pl.BlockSpec((1, tc), lambda i, j: (0, j)),
                  pl.BlockSpec((tr, REL_BUCKETS), lambda i, j: (i, 0))],
        out_specs=pl.BlockSpec((tr, tc), lambda i, j: (i, j)),
        out_shape=jax.ShapeDtypeStruct((r, c), F32),
        compiler_params=_params(("arbitrary", "arbitrary"), 32),
        name="t5_bias",
    )(col(qpos), row(kpos), col(qgrp), row(kgrp), rb_rows)


def _toeplitz_bias_kernel(rb_ref, o_ref, *, t, mult):
    off = pl.program_id(1)
    w = 2 * t
    u = lax.broadcasted_iota(jnp.int32, (1, w), 1)
    dist = off * t + jnp.where(u < t, -u, w - u)
    bucket = _t5_bucket(dist)
    rb = rb_ref[...]
    g = jnp.zeros((1, w), F32)
    for i in range(REL_BUCKETS):
        g = jnp.where(bucket == i, rb[:, i:i + 1], g)
    x = jnp.broadcast_to(g * mult, (t, w))
    o_ref[...] = pltpu.roll(x, 0, 1, stride=1, stride_axis=0)[:, :t]


def _toeplitz_bias(rel_bias, n_off, t, mult):
    n_heads = rel_bias.shape[1]
    return pl.pallas_call(
        functools.partial(_toeplitz_bias_kernel, t=t, mult=mult),
        grid=(n_heads, n_off),
        in_specs=[pl.BlockSpec((None, 1, REL_BUCKETS), lambda h, o: (h, 0, 0))],
        out_specs=pl.BlockSpec((None, None, t, t), lambda h, o: (h, o, 0, 0)),
        out_shape=jax.ShapeDtypeStruct((n_heads, n_off, t, t), F32),
        compiler_params=_params(("arbitrary", "arbitrary"), 32),
        name="t5_bias_tiles",
    )(rel_bias.T.reshape(n_heads, 1, REL_BUCKETS))


def _first_const_dist(limit):
    n = np.arange(1, limit + 1)
    large = 16 + (np.log(n.astype(np.float32) / np.float32(16)) / np.float32(math.log(8.0))
                  * np.float32(16)).astype(np.int32)
    bucket = np.where(n < 16, n, np.minimum(large, REL_BUCKETS - 1))
    not_last = np.nonzero(bucket != REL_BUCKETS - 1)[0]
    return int(n[not_last[-1]] + 1) if not_last.size else 1


def _lambda(lam_ref, lam_init):
    lp = lam_ref[...]
    a = jnp.sum(lp[0:1] * lp[1:2], axis=-1, keepdims=True)
    b = jnp.sum(lp[2:3] * lp[3:4], axis=-1, keepdims=True)
    return jnp.exp(a) - jnp.exp(b) + lam_init


def _flash_kernel(q_ref, k_ref, v_ref, bias_ref, lam_ref, ng_ref, o_ref,
                  kb_ref, vb_ref, m_ref, l_ref, acc_ref, *, t, hd, n_near, lam_init, scale):
    qi = pl.program_id(2)

    @pl.when(qi == 0)
    def _():
        kb_ref[...] = k_ref[...].astype(BF16)
        vb_ref[...] = v_ref[...].astype(BF16)

    q = q_ref[...] * (scale * LOG2E)
    lane = lax.broadcasted_iota(jnp.int32, q.shape, 1)
    q2 = jnp.concatenate([jnp.where(lane < hd, q, 0.0), jnp.where(lane >= hd, q, 0.0)],
                         axis=0).astype(BF16)
    m_ref[...] = jnp.full(m_ref.shape, NEG, F32)
    l_ref[...] = jnp.zeros(l_ref.shape, F32)
    acc_ref[...] = jnp.zeros(acc_ref.shape, F32)
    far_bias = bias_ref[n_near - 1, t - 1:t, 0:1]

    def tile(kj, bias, mask):
        rows = pl.ds(pl.multiple_of(kj * t, t), t)
        kt = kb_ref[rows, :]
        vt = vb_ref[rows, :]
        for c in range(2):
            rc = slice(c * t, (c + 1) * t)
            s = lax.dot_general(q2[rc], kt, (((1,), (1,)), ((), ())),
                                preferred_element_type=F32)
            s = s + bias
            if mask is not None:
                s = jnp.where(mask, s, NEG)
            m_old = m_ref[rc, :]
            m_new = jnp.maximum(m_old, jnp.max(s, axis=-1, keepdims=True))
            alpha = jnp.exp2(m_old - m_new)
            p = jnp.exp2(s - pltpu.repeat(m_new, t // LANES, axis=1))
            l_ref[rc, :] = alpha * l_ref[rc, :] + jnp.sum(p, axis=-1, keepdims=True)
            acc_ref[rc, :] = (pltpu.repeat(alpha, 2 * hd // LANES, axis=1) * acc_ref[rc, :]
                              + jnp.dot(p.astype(BF16), vt, preferred_element_type=F32))
            m_ref[rc, :] = m_new

    def far_body(kj, carry):
        tile(kj, far_bias, None)
        return carry

    lax.fori_loop(0, jnp.maximum(qi - (n_near - 1), 0), far_body, 0)
    for off in range(n_near - 1, 0, -1):
        @pl.when(qi >= off)
        def _(off=off):
            tile(qi - off, bias_ref[off], None)
    ri = lax.broadcasted_iota(jnp.int32, (t, t), 0)
    ci = lax.broadcasted_iota(jnp.int32, (t, t), 1)
    tile(qi, bias_ref[0], ri >= ci)

    lam = _lambda(lam_ref, lam_init)
    a = acc_ref[...] / pltpu.repeat(l_ref[...], 2 * hd // LANES, axis=1)
    o = a[:t] - lam * a[t:]
    y = o * lax.rsqrt(jnp.mean(o * o, axis=-1, keepdims=True) + LN_EPS) * ng_ref[...]
    o_ref[...] = (y * (1.0 - lam_init)).astype(BF16)


def _flash_diff(q, k, v, rel_bias, lam_params, norm_g, n_seq, lam_init):
    m, d = q.shape
    t_seq = m // n_seq
    hw = d // DIFF_HEADS
    hd = hw // 2
    t = next(c for c in range(FLASH_TILE, 0, -LANES) if t_seq % c == 0)
    nq = t_seq // t
    n_near = min(nq, -(-(_first_const_dist(t_seq) - 1) // t) + 1)
    bias = _toeplitz_bias(rel_bias, n_near, t, LOG2E)
    return pl.pallas_call(
        functools.partial(_flash_kernel, t=t, hd=hd, n_near=n_near, lam_init=lam_init, scale=hd ** -0.5),
        grid=(n_seq, DIFF_HEADS, nq),
        in_specs=[pl.BlockSpec((t, hw), lambda b, h, i: (b * nq + i, h)),
                  pl.BlockSpec((t_seq, hw), lambda b, h, i: (b, h)),
                  pl.BlockSpec((t_seq, hw), lambda b, h, i: (b, h)),
                  pl.BlockSpec((None, n_near, t, t), lambda b, h, i: (h, 0, 0, 0)),
                  pl.BlockSpec((4, hd), lambda b, h, i: (0, 0)),
                  pl.BlockSpec((1, hw), lambda b, h, i: (0, 0))],
        out_specs=pl.BlockSpec((t, hw), lambda b, h, i: (b * nq + i, h)),
        out_shape=jax.ShapeDtypeStruct((m, d), BF16),
        scratch_shapes=[pltpu.VMEM((t_seq, hw), BF16), pltpu.VMEM((t_seq, hw), BF16),
                        pltpu.VMEM((2 * t, LANES), F32), pltpu.VMEM((2 * t, LANES), F32),
                        pltpu.VMEM((2 * t, hw), F32)],
        compiler_params=_params(("arbitrary", "arbitrary", "arbitrary"), 48),
        name="flash_diff",
    )(q, k, v, bias, lam_params, norm_g.reshape(1, hw))


def _paged_kernel(pt_ref, q_ref, kn_ref, vn_ref, tab_ref, tabn_ref, lam_ref, ng_ref, *rest,
                  pages, q_tok, lam_init, scale):
    k_refs = rest[:pages]
    v_refs = rest[pages:2 * pages]
    o_ref, m_ref, l_ref, acc_ref = rest[2 * pages:]
    s_id = pl.program_id(1)
    rows, hw = q_ref.shape
    half = rows // 2
    q = q_ref[...]

    def scores(kmat):
        return lax.dot_general(q, kmat, (((1,), (1,)), ((), ())), preferred_element_type=F32) * scale

    def update(s, vmat):
        m_old = m_ref[...]
        m_new = jnp.maximum(m_old, jnp.max(s, axis=-1, keepdims=True))
        alpha = jnp.exp(m_old - m_new)
        p = jnp.exp(s - pltpu.repeat(m_new, s.shape[1] // LANES, axis=1))
        l_ref[...] = alpha * l_ref[...] + jnp.sum(p, axis=-1, keepdims=True)
        acc_ref[...] = (pltpu.repeat(alpha, hw // LANES, axis=1) * acc_ref[...]
                        + jnp.dot(p.astype(BF16), vmat, preferred_element_type=F32))
        m_ref[...] = m_new

    @pl.when(s_id == 0)
    def _():
        m_ref[...] = jnp.full(m_ref.shape, NEG, F32)
        l_ref[...] = jnp.zeros(l_ref.shape, F32)
        acc_ref[...] = jnp.zeros(acc_ref.shape, F32)
        s = scores(kn_ref[...].astype(BF16)) + tabn_ref[...]
        tq = lax.broadcasted_iota(jnp.int32, s.shape, 0) % q_tok
        tk = lax.broadcasted_iota(jnp.int32, s.shape, 1) // DIFF_HEADS
        update(jnp.where(tk <= tq, s, NEG), vn_ref[...].astype(BF16))

    kp = jnp.concatenate([r[...].astype(BF16) for r in k_refs], axis=0)
    vp = jnp.concatenate([r[...].astype(BF16) for r in v_refs], axis=0)
    update(scores(kp) + tab_ref[...], vp)

    @pl.when(s_id == pl.num_programs(1) - 1)
    def _():
        a = acc_ref[...] / pltpu.repeat(l_ref[...], hw // LANES, axis=1)
        lam = _lambda(lam_ref, lam_init)
        o = a[:half] - lam * a[half:]
        y = o * lax.rsqrt(jnp.mean(o * o, axis=-1, keepdims=True) + LN_EPS) * ng_ref[...]
        o_ref[...] = (y * (1.0 - lam_init)).astype(o_ref.dtype)


def _paged_diff(q, k_new, v_new, cache_k, cache_v, page_table, layer, rel_bias, lam_params, norm_g,
                lam_init):
    nb, tq, d = q.shape
    tk = k_new.shape[1]
    assert tq <= tk
    hw = d // DIFF_HEADS
    hd = hw // 2
    n_pages = page_table.shape[1]
    past = n_pages * PAGE_SIZE
    pages = next(p for p in (PAGES_PER_STEP, 2, 1) if n_pages % p == 0)
    n_steps = n_pages // pages
    rows = 2 * DIFF_HEADS * tq
    page_rows = PAGE_SIZE * DIFF_HEADS
    assert pages * PAGE_SIZE + 1 >= _first_const_dist(past + tq)
    q5 = q.reshape(nb, tq, DIFF_HEADS, 2, hd).transpose(0, 3, 2, 1, 4)
    same_c = jnp.arange(2)[:, None] == jnp.arange(2)[None, :]
    qd = jnp.where(same_c[None, :, None, None, :, None], q5[:, :, :, :, None, :], 0.0)
    qd = qd.reshape(nb, rows, hw).astype(BF16)
    k_rows = k_new.reshape(nb, tk * DIFF_HEADS, hw)
    v_rows = v_new.reshape(nb, tk * DIFF_HEADS, hw)
    qpos = jnp.tile(past + jnp.arange(tq), 2 * DIFF_HEADS)
    qgrp = jnp.tile(jnp.repeat(jnp.arange(DIFF_HEADS), tq), 2)
    rb_rows = jnp.tile(jnp.repeat(rel_bias.T, tq, axis=0), (2, 1))
    col_tok = jnp.repeat(jnp.arange(pages * PAGE_SIZE), DIFF_HEADS)
    col_grp = jnp.tile(jnp.arange(DIFF_HEADS), pages * PAGE_SIZE)
    kpos = jnp.concatenate([jnp.zeros_like(col_tok), (n_steps - 1) * pages * PAGE_SIZE + col_tok])
    tab = _bias_table(qpos, kpos, rb_rows, rows, page_rows, qgrp, jnp.tile(col_grp, 2))
    tab = tab.reshape(rows, 2, pages * page_rows).transpose(1, 0, 2)
    tab_new = _bias_table(qpos, past + jnp.repeat(jnp.arange(tk), DIFF_HEADS), rb_rows, rows,
                          tk * DIFF_HEADS, qgrp, jnp.tile(jnp.arange(DIFF_HEADS), tk))
    ck = cache_k.reshape(cache_k.shape[0], cache_k.shape[1], page_rows, hw)
    cv = cache_v.reshape(cache_v.shape[0], cache_v.shape[1], page_rows, hw)

    def page_spec(i):
        return pl.BlockSpec((None, None, page_rows, hw),
                            lambda b, s, pt: (layer, pt[b, s * pages + i], 0, 0))

    grid_spec = pltpu.PrefetchScalarGridSpec(
        num_scalar_prefetch=1,
        grid=(nb, n_steps),
        in_specs=[pl.BlockSpec((None, rows, hw), lambda b, s, pt: (b, 0, 0)),
                  pl.BlockSpec((None, tk * DIFF_HEADS, hw), lambda b, s, pt: (b, 0, 0)),
                  pl.BlockSpec((None, tk * DIFF_HEADS, hw), lambda b, s, pt: (b, 0, 0)),
                  pl.BlockSpec((None, rows, pages * page_rows),
                               lambda b, s, pt: (jnp.where(s == n_steps - 1, 1, 0), 0, 0)),
                  pl.BlockSpec((rows, tk * DIFF_HEADS), lambda b, s, pt: (0, 0)),
                  pl.BlockSpec((4, hd), lambda b, s, pt: (0, 0)),
                  pl.BlockSpec((1, hw), lambda b, s, pt: (0, 0))]
                 + [page_spec(i) for i in range(pages)] + [page_spec(i) for i in range(pages)],
        out_specs=pl.BlockSpec((None, rows // 2, hw), lambda b, s, pt: (b, 0, 0)),
        scratch_shapes=[pltpu.VMEM((rows, LANES), F32), pltpu.VMEM((rows, LANES), F32),
                        pltpu.VMEM((rows, hw), F32)],
    )
    out = pl.pallas_call(
        functools.partial(_paged_kernel, pages=pages, q_tok=tq, lam_init=lam_init, scale=hd ** -0.5),
        grid_spec=grid_spec,
        out_shape=jax.ShapeDtypeStruct((nb, rows // 2, hw), BF16),
        compiler_params=_params(("arbitrary", "arbitrary"), 56),
        name="paged_diff",
    )(page_table, qd, k_rows, v_rows, tab, tab_new, lam_params, norm_g.reshape(1, hw),
      *([ck] * pages), *([cv] * pages))
    return out.reshape(nb, DIFF_HEADS, tq, hw).transpose(0, 2, 1, 3).reshape(nb, tq, d)


def kernel(x_prompt, x_sample, state_gla, cache_k, cache_v, page_table, ln_g, ln_b, ffn_w_gate, ffn_w_up,
           ffn_w_down, gla_w_in, gla_w_a2, gla_b_a, gla_norm_g, gla_w_o, diff_w_in, diff_lambda, diff_norm_g,
           diff_w_o, rel_bias):
    nb, t, d = x_prompt.shape
    sb, st, _ = x_sample.shape
    hw = d // DIFF_HEADS
    assert st <= SAMPLE_KEY_ROWS <= SAMPLE_ROWS
    xs = jnp.pad(x_sample, ((0, 0), (0, SAMPLE_ROWS - st), (0, 0))).reshape(sb * SAMPLE_ROWS, d)
    x0 = (x_prompt.reshape(nb * t, d), xs)
    xf = ("plain", x0)
    xb = tuple(_to_bf16(a) for a in x0)
    gla_p, gla_s, k_rows, v_rows = [], [], [], []

    def layer_norm(z, g, b):
        outs = [_layer_norm(a, g, b) for a in z]
        xb, mean, rstd = (tuple(o[i] for o in outs) for i in range(3))
        return ("ln", z, mean, rstd, g, b), xb

    def ffn(xf, xb, i, s):
        h = _ffn_up(xb, ffn_w_gate, ffn_w_up, (i, s))
        z = _res_mm(h, ffn_w_down, (i, s), xf, 0.5)
        if (i, s) == (DEPTH - 1, 1):
            return z, None
        return layer_norm(z, ln_g[i, 2 * s], ln_b[i, 2 * s])

    def sample_seq(a, rows):
        return a.reshape(sb, SAMPLE_ROWS, a.shape[-1])[:, :rows]

    def sample_rows(a):
        return jnp.pad(a, ((0, 0), (0, SAMPLE_ROWS - a.shape[1]), (0, 0))).reshape(sb * SAMPLE_ROWS, a.shape[-1])

    for i in range(DEPTH):
        xf, xb = ffn(xf, xb, i, 0)
        j = i // N_MIXERS
        if i % N_MIXERS == 0:
            hk = gla_w_a2.shape[-1]
            hv = gla_w_o.shape[1]
            qkvr = _proj(xb, gla_w_in, (j,), 0, 2 * hk + 2 * hv, 512)
            w_g = jnp.pad(gla_w_in[j][:, 2 * hk + 2 * hv:], ((0, 0), (0, LANES - GLA_GATE_RANK)))
            glr = _proj(xb, w_g[None], (0,), 0, LANES, LANES)
            w_a2p = jnp.pad(gla_w_a2[j], ((0, LANES - GLA_GATE_RANK), (0, 0)))
            o_p, s_p = _gla_core(qkvr[0], glr[0], w_a2p, gla_b_a[j], gla_norm_g[j], None, nb, t)
            chunk = lambda a: jnp.pad(sample_seq(a, SAMPLE_ROWS), ((0, 0), (0, GLA_CHUNK - SAMPLE_ROWS), (0, 0))
                                      ).reshape(sb * GLA_CHUNK, a.shape[-1])
            o_s, s_s = _gla_core(chunk(qkvr[1]), chunk(glr[1]), w_a2p, gla_b_a[j], gla_norm_g[j],
                                 state_gla[j], sb, st)
            o_s = o_s.reshape(sb, GLA_CHUNK, hv)[:, :SAMPLE_ROWS].reshape(sb * SAMPLE_ROWS, hv)
            gla_p.append(s_p)
            gla_s.append(s_s)
            z = _res_mm((o_p, o_s), gla_w_o, (j,), xf, 1.0)
        else:
            lam_init = 0.8 - 0.6 * math.exp(-0.3 * i)
            q = _proj(xb, diff_w_in, (j,), 0, d, 512)
            k = _proj(xb, diff_w_in, (j,), d // 512, d, 512)
            v = _proj(xb, diff_w_in, (j,), 2 * d // 512, d, 512)
            k_rows.append(k)
            v_rows.append(v)
            o_p = _flash_diff(q[0], k[0], v[0], rel_bias, diff_lambda[j], diff_norm_g[j], nb, lam_init)
            o_s = _paged_diff(sample_seq(q[1], st), sample_seq(k[1], SAMPLE_KEY_ROWS),
                              sample_seq(v[1], SAMPLE_KEY_ROWS), cache_k, cache_v, page_table, j, rel_bias,
                              diff_lambda[j], diff_norm_g[j], lam_init)
            z = _res_mm((o_p, sample_rows(o_s)), diff_w_o, (j,), xf, 1.0)
        xf, xb = layer_norm(z, ln_g[i, 1], ln_b[i, 1])
        xf, xb = ffn(xf, xb, i, 1)

    y = [_layer_norm(a, ln_g[DEPTH - 1, 2], ln_b[DEPTH - 1, 2], final=True)[0] for a in xf]
    heads = lambda a, n_seq, rows: a.reshape(n_seq, rows, DIFF_HEADS, hw)
    return (y[0].reshape(nb, t, d),
            sample_seq(y[1], st),
            jnp.stack(gla_p),
            jnp.stack(gla_s),
            jnp.stack([heads(k[0], nb, t) for k in k_rows]),
            jnp.stack([heads(v[0], nb, t) for v in v_rows]),
            jnp.stack([heads(sample_seq(k[1], st), sb, st) for k in k_rows]),
            jnp.stack([heads(sample_seq(v[1], st), sb, st) for v in v_rows]))
```
